```python
import jax, jax.numpy as jnp
from jax import lax
import numpy as np

D_MODEL = 1024
BATCH = 2
SEQ = 16384
DEPTH = 1
DEC_BATCH = 8
DEC_SEQ = 2048
PAST_LEN = 128

N_META = 16
GLA_HEADS = 4
GLA_DK = 64
GLA_DV = 128
GLA_GATE_RANK = 16
GLA_GATE_TAU = 16.0
GLA_CHUNK = 64
MLA_HEADS = 8
MLA_Q_RANK = 256
MLA_KV_RANK = 128
MLA_NOPE = 64
MLA_ROPE = 32
MLA_DV = 64
ROPE_BASE = 10000.0
Q_BLOCK = 128
IN_SIZES = (GLA_HEADS * GLA_DK, GLA_HEADS * GLA_DK, GLA_HEADS * GLA_DV, GLA_GATE_RANK, GLA_GATE_RANK,
            GLA_HEADS * GLA_DV, MLA_Q_RANK, MLA_KV_RANK, MLA_ROPE)
D_IN = sum(IN_SIZES)
MIX_WIDTH = GLA_HEADS * GLA_DV + MLA_HEADS * MLA_DV
N_EXPERTS = 32
TOP_K = 4
D_FF = D_MODEL
SWIGLU_LIMIT = 7.0
SWIGLU_ALPHA = 1.702
MOE_BLOCK = 256
ALPHA = (2.0 * DEPTH) ** 0.25
BETA = (8.0 * DEPTH) ** -0.25
LN_EPS = 1e-5
RMS_EPS = 1e-6

kernel_name = 'hymba_gla_mla_moe_deepnorm_encoder'


def layer_norm(x, g, b):
    xf = x.astype(jnp.float32)
    mu = jnp.mean(xf, axis=-1, keepdims=True)
    xc = xf - mu
    var = jnp.mean(xc * xc, axis=-1, keepdims=True)
    return (xc * lax.rsqrt(var + LN_EPS) * g.astype(jnp.float32) + b.astype(jnp.float32)).astype(x.dtype)


def rms_norm(x, g):
    xf = x.astype(jnp.float32)
    ms = jnp.mean(xf * xf, axis=-1, keepdims=True)
    return (xf * lax.rsqrt(ms + RMS_EPS) * g.astype(jnp.float32)).astype(x.dtype)


def split_cols(a, sizes):
    out, start = [], 0
    for s in sizes:
        out.append(a[..., start:start + s])
        start += s
    return out


def rope_tables(L):
    inv = ROPE_BASE ** (-jnp.arange(0, MLA_ROPE, 2, dtype=jnp.float32) / MLA_ROPE)
    ang = jnp.arange(L, dtype=jnp.float32)[:, None] * inv[None, :]
    return jnp.cos(ang), jnp.sin(ang)


def apply_rope(x, cos, sin):
    half = x.shape[-1] // 2
    x1, x2 = x[..., :half], x[..., half:]
    cos = cos.astype(x.dtype)
    sin = sin.astype(x.dtype)
    return jnp.concatenate([x1 * cos - x2 * sin, x1 * sin + x2 * cos], axis=-1)


def gla_direction(q, k, v, lg, reverse, exclusive):
    B, L, H, _ = q.shape
    C = GLA_CHUNK
    front = C - N_META
    back = (-(L + front)) % C
    padw = ((0, 0), (front, back), (0, 0), (0, 0))
    q, k, v, lg = [jnp.pad(a.astype(jnp.float32), padw) for a in (q, k, v, lg)]
    if reverse:
        q, k, v, lg = [jnp.flip(a, axis=1) for a in (q, k, v, lg)]
    Lp = L + front + back
    N = Lp // C

    def chunks(a):
        return a.reshape(B, N, C, H, a.shape[-1]).transpose(1, 0, 3, 2, 4)

    qc, kc, vc, lgc = chunks(q), chunks(k), chunks(v), chunks(lg)
    bc = jnp.cumsum(lgc, axis=3)
    mask = jnp.tril(jnp.ones((C, C), dtype=bool), k=-1 if exclusive else 0)

    def step(S, inp):
        qt, kt, vt, bt = inp
        diff = bt[:, :, :, None, :] - bt[:, :, None, :, :]
        decay = jnp.exp(jnp.where(mask[:, :, None], diff, -jnp.inf))
        A = jnp.einsum('bhtd,bhsd,bhtsd->bhts', qt, kt, decay)
        o = jnp.einsum('bhts,bhsv->bhtv', A, vt) + jnp.einsum('bhtd,bhdv->bhtv', qt * jnp.exp(bt), S)
        bl = bt[:, :, -1:, :]
        S = jnp.exp(bl[:, :, 0, :])[..., None] * S + jnp.einsum('bhsd,bhsv->bhdv', kt * jnp.exp(bl - bt), vt)
        return S, o

    S0 = jnp.zeros((B, H, q.shape[-1], v.shape[-1]), jnp.float32)
    _, o = lax.scan(step, S0, (qc, kc, vc, bc))
    o = o.transpose(1, 0, 3, 2, 4).reshape(B, Lp, H, v.shape[-1])
    if reverse:
        o = jnp.flip(o, axis=1)
    return o[:, front:front + L]


def gla_group(q_g, k_g, v_g, lr_f, lr_b, g_out, gate_w_f, gate_b_f, gate_w_b, gate_b_b, norm_g):
    B, L, _ = q_g.shape
    q = q_g.reshape(B, L, GLA_HEADS, GLA_DK) * (GLA_DK ** -0.5)
    k = k_g.reshape(B, L, GLA_HEADS, GLA_DK)
    v = v_g.reshape(B, L, GLA_HEADS, GLA_DV)
    lg_f = (jax.nn.log_sigmoid((lr_f @ gate_w_f + gate_b_f).astype(jnp.float32)) / GLA_GATE_TAU).reshape(B, L, GLA_HEADS, GLA_DK)
    lg_b = (jax.nn.log_sigmoid((lr_b @ gate_w_b + gate_b_b).astype(jnp.float32)) / GLA_GATE_TAU).reshape(B, L, GLA_HEADS, GLA_DK)
    o = gla_direction(q, k, v, lg_f, reverse=False, exclusive=False) + gla_direction(q, k, v, lg_b, reverse=True, exclusive=True)
    o = rms_norm(o, norm_g.reshape(GLA_HEADS, GLA_DV)).astype(q_g.dtype)
    o = o * jax.nn.silu(g_out).reshape(B, L, GLA_HEADS, GLA_DV)
    return o.reshape(B, L, GLA_HEADS * GLA_DV)


def mla_group(c_q, c_kv, k_r, q_norm_g, w_uq, kv_norm_g, w_ukv):
    B, L, _ = c_q.shape
    q = (rms_norm(c_q, q_norm_g) @ w_uq).reshape(B, L, MLA_HEADS, MLA_NOPE + MLA_ROPE)
    q_nope, q_rope = q[..., :MLA_NOPE], q[..., MLA_NOPE:]
    kv = (rms_norm(c_kv, kv_norm_g) @ w_ukv).reshape(B, L, MLA_HEADS, MLA_NOPE + MLA_DV)
    k_nope, v = kv[..., :MLA_NOPE], kv[..., MLA_NOPE:]
    cos, sin = rope_tables(L)
    q_rope = apply_rope(q_rope, cos[:, None, :], sin[:, None, :])
    k_rope = apply_rope(k_r, cos, sin)
    scale = (MLA_NOPE + MLA_ROPE) ** -0.5
    n_blk = -(-L // Q_BLOCK)
    pad = n_blk * Q_BLOCK - L

    def qblocks(a):
        a = jnp.pad(a, ((0, 0), (0, pad), (0, 0), (0, 0)))
        return a.reshape(B, n_blk, Q_BLOCK, MLA_HEADS, a.shape[-1]).transpose(1, 0, 2, 3, 4)

    def block(args):
        qn, qr = args
        s = jnp.einsum('bqhd,bkhd->bhqk', qn, k_nope) + jnp.einsum('bqhr,bkr->bhqk', qr, k_rope)
        p = jax.nn.softmax(s.astype(jnp.float32) * scale, axis=-1).astype(v.dtype)
        return jnp.einsum('bhqk,bkhd->bqhd', p, v)

    o = lax.map(block, (qblocks(q_nope), qblocks(q_rope)))
    o = o.transpose(1, 0, 2, 3, 4).reshape(B, n_blk * Q_BLOCK, MLA_HEADS * MLA_DV)
    return o[:, :L]


def token_mixer(x, w_in, gate_w_f, gate_b_f, gate_w_b, gate_b_b, gla_norm_g,
                q_norm_g, w_uq, kv_norm_g, w_ukv, w_out):
    proj = x @ w_in
    q_g, k_g, v_g, lr_f, lr_b, g_out, c_q, c_kv, k_r = split_cols(proj, IN_SIZES)
    gla = gla_group(q_g, k_g, v_g, lr_f, lr_b, g_out, gate_w_f, gate_b_f, gate_w_b, gate_b_b, gla_norm_g)
    mla = mla_group(c_q, c_kv, k_r, q_norm_g, w_uq, kv_norm_g, w_ukv)
    return jnp.concatenate([gla, mla], axis=-1) @ w_out


def moe(h, router_w, router_b, w_gu, b_gu, w_down, b_down):
    T, D = h.shape
    logits = (h @ router_w + router_b).astype(jnp.float32)
    top_logits, top_idx = lax.top_k(logits, TOP_K)
    gates = jax.nn.softmax(top_logits, axis=-1)
    n_assign = T * TOP_K
    flat_e = top_idx.reshape(-1).astype(jnp.int32)
    flat_tok = jnp.repeat(jnp.arange(T, dtype=jnp.int32), TOP_K)
    flat_gate = gates.reshape(-1)
    order = jnp.argsort(flat_e)
    se = flat_e[order]
    counts = jnp.bincount(flat_e, length=N_EXPERTS)
    starts = jnp.cumsum(counts) - counts
    pcounts = (counts + MOE_BLOCK - 1) // MOE_BLOCK * MOE_BLOCK
    pends = jnp.cumsum(pcounts)
    pstarts = pends - pcounts
    dest = pstarts[se] + jnp.arange(n_assign, dtype=jnp.int32) - starts[se]
    n_blocks = -(-n_assign // MOE_BLOCK) + N_EXPERTS
    slots = n_blocks * MOE_BLOCK
    slot_tok = jnp.zeros((slots,), jnp.int32).at[dest].set(flat_tok[order])
    slot_gate = jnp.zeros((slots,), jnp.float32).at[dest].set(flat_gate[order])
    block_e = jnp.minimum(jnp.searchsorted(pends, jnp.arange(n_blocks, dtype=pends.dtype) * MOE_BLOCK, side='right'),
                          N_EXPERTS - 1)

    def run(args):
        tok, e = args
        xb = h[tok]
        gu = xb @ w_gu[e] + b_gu[e]
        gate, up = gu[:, :D_FF], gu[:, D_FF:]
        gate = jnp.minimum(gate, SWIGLU_LIMIT)
        up = jnp.clip(up, -SWIGLU_LIMIT, SWIGLU_LIMIT)
        act = (up + 1) * gate * jax.nn.sigmoid(SWIGLU_ALPHA * gate)
        return act @ w_down[e] + b_down[e]

    out = lax.map(run, (slot_tok.reshape(n_blocks, MOE_BLOCK), block_e))
    out = out.reshape(slots, D) * slot_gate[:, None].astype(h.dtype)
    return jnp.zeros_like(h).at[slot_tok].add(out)


def encoder_trunk(x, meta_tokens, ln_in_g, ln_in_b, w_in, gla_gate_w_fwd, gla_gate_b_fwd, gla_gate_w_bwd,
                  gla_gate_b_bwd, gla_norm_g, mla_q_norm_g, mla_w_uq, mla_kv_norm_g, mla_w_ukv, w_out,
                  ln1_g, ln1_b, router_w, router_b, expert_w_gu, expert_b_gu, expert_w_down, expert_b_down,
                  ln2_g, ln2_b):
    B = x.shape[0]
    meta = jnp.broadcast_to(meta_tokens.astype(x.dtype)[None], (B, N_META, D_MODEL))
    h = layer_norm(jnp.concatenate([meta, x], axis=1), ln_in_g, ln_in_b)
    for l in range(DEPTH):
        mix = token_mixer(h, w_in[l], gla_gate_w_fwd[l], gla_gate_b_fwd[l], gla_gate_w_bwd[l], gla_gate_b_bwd[l],
                          gla_norm_g[l], mla_q_norm_g[l], mla_w_uq[l], mla_kv_norm_g[l], mla_w_ukv[l], w_out[l])
        h = layer_norm(ALPHA * h + mix, ln1_g[l], ln1_b[l])
        ffn = moe(h.reshape(-1, D_MODEL), router_w[l], router_b[l], expert_w_gu[l], expert_b_gu[l],
                  expert_w_down[l], expert_b_down[l]).reshape(h.shape)
        h = layer_norm(ALPHA * h + ffn, ln2_g[l], ln2_b[l])
    return h[:, N_META:]


def setup_inputs(seed: int = 0) -> dict:
    key = jax.random.key(seed)
    ks = jax.random.split(key, 32)
    f32 = jnp.float32

    def nrm(k, shape, scale):
        return jax.random.normal(k, shape, f32) * scale

    def gain(k, shape):
        return 1.0 + 0.02 * jax.random.normal(k, shape, f32)

    D = D_MODEL
    return {
        'x_prompt': nrm(ks[0], (BATCH, SEQ, D), 1.0),
        'x_sample': nrm(ks[1], (DEC_BATCH, DEC_SEQ, D), 1.0),
        'meta_tokens': nrm(ks[2], (N_META, D), 1.0),
        'ln_in_g': gain(ks[3], (D,)),
        'ln_in_b': nrm(ks[4], (D,), 0.02),
        'w_in': nrm(ks[5], (DEPTH, D, D_IN), D ** -0.5),
        'gla_gate_w_fwd': nrm(ks[6], (DEPTH, GLA_GATE_RANK, GLA_HEADS * GLA_DK), GLA_GATE_RANK ** -0.5),
        'gla_gate_b_fwd': nrm(ks[7], (DEPTH, GLA_HEADS * GLA_DK), 0.1),
        'gla_gate_w_bwd': nrm(ks[8], (DEPTH, GLA_GATE_RANK, GLA_HEADS * GLA_DK), GLA_GATE_RANK ** -0.5),
        'gla_gate_b_bwd': nrm(ks[9], (DEPTH, GLA_HEADS * GLA_DK), 0.1),
        'gla_norm_g': gain(ks[10], (DEPTH, GLA_HEADS * GLA_DV)),
        'mla_q_norm_g': gain(ks[11], (DEPTH, MLA_Q_RANK)),
        'mla_w_uq': nrm(ks[12], (DEPTH, MLA_Q_RANK, MLA_HEADS * (MLA_NOPE + MLA_ROPE)), MLA_Q_RANK ** -0.5),
        'mla_kv_norm_g': gain(ks[13], (DEPTH, MLA_KV_RANK)),
        'mla_w_ukv': nrm(ks[14], (DEPTH, MLA_KV_RANK, MLA_HEADS * (MLA_NOPE + MLA_DV)), MLA_KV_RANK ** -0.5),
        'w_out': nrm(ks[15], (DEPTH, MIX_WIDTH, D), MIX_WIDTH ** -0.5 * BETA),
        'ln1_g': gain(ks[16], (DEPTH, D)),
        'ln1_b': nrm(ks[17], (DEPTH, D), 0.02),
        'router_w': nrm(ks[18], (DEPTH, D, N_EXPERTS), D ** -0.5),
        'router_b': nrm(ks[19], (DEPTH, N_EXPERTS), 0.01),
        'expert_w_gu': nrm(ks[20], (DEPTH, N_EXPERTS, D, 2 * D_FF), D ** -0.5),
        'expert_b_gu': nrm(ks[21], (DEPTH, N_EXPERTS, 2 * D_FF), 0.01),
        'expert_w_down': nrm(ks[22], (DEPTH, N_EXPERTS, D_FF, D), D_FF ** -0.5 * BETA),
        'expert_b_down': nrm(ks[23], (DEPTH, N_EXPERTS, D), 0.01),
        'ln2_g': gain(ks[24], (DEPTH, D)),
        'ln2_b': nrm(ks[25], (DEPTH, D), 0.02),
    }


def reference(x_prompt, x_sample, meta_tokens, ln_in_g, ln_in_b, w_in, gla_gate_w_fwd, gla_gate_b_fwd,
              gla_gate_w_bwd, gla_gate_b_bwd, gla_norm_g, mla_q_norm_g, mla_w_uq, mla_kv_norm_g, mla_w_ukv,
              w_out, ln1_g, ln1_b, router_w, router_b, expert_w_gu, expert_b_gu, expert_w_down, expert_b_down,
              ln2_g, ln2_b):
    y_prompt = encoder_trunk(x_prompt, meta_tokens, ln_in_g, ln_in_b, w_in, gla_gate_w_fwd, gla_gate_b_fwd,
                             gla_gate_w_bwd, gla_gate_b_bwd, gla_norm_g, mla_q_norm_g, mla_w_uq, mla_kv_norm_g,
                             mla_w_ukv, w_out, ln1_g, ln1_b, router_w, router_b, expert_w_gu, expert_b_gu,
                             expert_w_down, expert_b_down, ln2_g, ln2_b)
    y_sample = encoder_trunk(x_sample, meta_tokens, ln_in_g, ln_in_b, w_in, gla_gate_w_fwd, gla_gate_b_fwd,
                             gla_gate_w_bwd, gla_gate_b_bwd, gla_norm_g, mla_q_norm_g, mla_w_uq, mla_kv_norm_g,
                             mla_w_ukv, w_out, ln1_g, ln1_b, router_w, router_b, expert_w_gu, expert_b_gu,
                             expert_w_down, expert_b_down, ln2_g, ln2_b)
    return (y_prompt, y_sample)
```

```python
import functools
import math

import jax
import jax.numpy as jnp
from jax import lax
from jax.experimental import pallas as pl
from jax.experimental.pallas import tpu as pltpu

D_MODEL = 1024
N_META = 16
GLA_HEADS, GLA_DK, GLA_DV, GLA_RANK = 4, 64, 128, 16
GLA_TAU = 16.0
GLA_CHUNK = 64
MLA_HEADS, MLA_QR, MLA_KVR, MLA_NOPE, MLA_ROPE, MLA_DV = 8, 256, 128, 64, 32, 64
ROPE_BASE = 10000.0
N_EXPERTS, TOP_K, D_FF = 32, 4, 1024
SWIGLU_LIMIT, SWIGLU_ALPHA = 7.0, 1.702
DEPTH = 1
ALPHA = (2.0 * DEPTH) ** 0.25
LN_EPS, RMS_EPS = 1e-5, 1e-6

LANE = 128
HEAD_PAD = 128
GK = GLA_HEADS * GLA_DK
GV = GLA_HEADS * GLA_DV
MV = MLA_HEADS * MLA_DV
MQ = MLA_HEADS * HEAD_PAD
C_Q, C_K, C_V, C_G, C_CQ, C_CKV, C_KRA, C_KRB, C_LR, C_END = (
    0, 256, 512, 1024, 1536, 1792, 1920, 2048, 2176, 2304)
MOE_BLOCK = 256
NEG_BIG = -1e30
VMEM_LIMIT = 56 * 1024 * 1024

BF16 = jnp.bfloat16
F32 = jnp.float32


def _dot(a, b):
    return jnp.dot(a, b, preferred_element_type=F32)


def _dot_nt(a, b):
    return lax.dot_general(a, b, (((1,), (1,)), ((), ())), preferred_element_type=F32)


def _dot_tn(a, b):
    return lax.dot_general(a, b, (((0,), (0,)), ((), ())), preferred_element_type=F32)


def _layer_norm(x, g, b):
    mu = jnp.mean(x, axis=-1, keepdims=True)
    xc = x - mu
    var = jnp.mean(xc * xc, axis=-1, keepdims=True)
    return xc * lax.rsqrt(var + LN_EPS) * g + b


def _rms_norm(x, g):
    ms = jnp.mean(x * x, axis=-1, keepdims=True)
    return x * lax.rsqrt(ms + RMS_EPS) * g


def _log_sigmoid(z):
    return jnp.minimum(z, 0.0) - jnp.log(1.0 + jnp.exp(-jnp.abs(z)))


def _in_proj_kernel(x_ref, lng_ref, lnb_ref, w1_ref, wg_ref, bg_ref, qng_ref, kvng_ref,
                    waT_ref, wbT_ref, wk_ref, wvT_ref, cosqT_ref, sinqT_ref, cosk_ref, sink_ref,
                    qkv_ref, lg_ref, g_ref, qT_ref, k_ref, vT_ref):
    xn = _layer_norm(x_ref[...], lng_ref[...], lnb_ref[...])
    p = _dot(xn.astype(BF16), w1_ref[...])
    qkv_ref[...] = p[:, C_Q:C_G]
    g_ref[...] = p[:, C_G:C_CQ].astype(BF16)
    z = _dot(p[:, C_LR:C_END].astype(BF16), wg_ref[...]) + bg_ref[...]
    lg_ref[...] = _log_sigmoid(z) * (1.0 / GLA_TAU)
    cqn = _rms_norm(p[:, C_CQ:C_CKV], qng_ref[...]).astype(BF16)
    qaT = _dot_nt(waT_ref[...], cqn)
    qbT = _dot_nt(wbT_ref[...], cqn)
    cosq, sinq = cosqT_ref[...], sinqT_ref[...]
    for h in range(MLA_HEADS):
        sl = slice(h * HEAD_PAD, (h + 1) * HEAD_PAD)
        qT_ref[sl, :] = (qaT[sl, :] * cosq + qbT[sl, :] * sinq).astype(BF16)
    ckvn = _rms_norm(p[:, C_CKV:C_KRA], kvng_ref[...]).astype(BF16)
    knp = _dot(ckvn, wk_ref[...])
    krope = p[:, C_KRA:C_KRB] * cosk_ref[...] + p[:, C_KRB:C_LR] * sink_ref[...]
    for h in range(MLA_HEADS):
        sl = slice(h * HEAD_PAD, (h + 1) * HEAD_PAD)
        k_ref[:, sl] = (knp[:, sl] + krope).astype(BF16)
    vT_ref[...] = _dot_nt(wvT_ref[...], ckvn).astype(BF16)


def _in_proj(x, wts, tabs, tm):
    B, S, D = x.shape
    n = S // tm
    const = lambda shape: pl.BlockSpec(shape, lambda b, i: (0,) * len(shape))
    in_specs = [
        pl.BlockSpec((None, tm, D), lambda b, i: (b, i, 0)),
        const((1, D)), const((1, D)),
        const((D, C_END)), const((LANE, 2 * GK)), const((1, 2 * GK)),
        const((1, MLA_QR)), const((1, MLA_KVR)),
        const((MQ, MLA_QR)), const((MQ, MLA_QR)),
        const((MLA_KVR, MQ)), const((MV, MLA_KVR)),
        pl.BlockSpec((HEAD_PAD, tm), lambda b, i: (0, i)),
        pl.BlockSpec((HEAD_PAD, tm), lambda b, i: (0, i)),
        pl.BlockSpec((tm, HEAD_PAD), lambda b, i: (i, 0)),
        pl.BlockSpec((tm, HEAD_PAD), lambda b, i: (i, 0)),
    ]
    out_shape = [
        jax.ShapeDtypeStruct((B, S, C_G), F32),
        jax.ShapeDtypeStruct((B, S, 2 * GK), F32),
        jax.ShapeDtypeStruct((B, S, GV), BF16),
        jax.ShapeDtypeStruct((B, n, MQ, tm), BF16),
        jax.ShapeDtypeStruct((B, S, MQ), BF16),
        jax.ShapeDtypeStruct((B, n, MV, tm), BF16),
    ]
    out_specs = [
        pl.BlockSpec((None, tm, C_G), lambda b, i: (b, i, 0)),
        pl.BlockSpec((None, tm, 2 * GK), lambda b, i: (b, i, 0)),
        pl.BlockSpec((None, tm, GV), lambda b, i: (b, i, 0)),
        pl.BlockSpec((None, None, MQ, tm), lambda b, i: (b, i, 0, 0)),
        pl.BlockSpec((None, tm, MQ), lambda b, i: (b, i, 0)),
        pl.BlockSpec((None, None, MV, tm), lambda b, i: (b, i, 0, 0)),
    ]
    return pl.pallas_call(
        _in_proj_kernel, grid=(B, n), in_specs=in_specs, out_specs=out_specs, out_shape=out_shape,
        compiler_params=pltpu.CompilerParams(
            dimension_semantics=("parallel", "parallel"), vmem_limit_bytes=VMEM_LIMIT),
        name="in_proj",
    )(x, wts["ln_in_g"], wts["ln_in_b"], wts["w1"], wts["wg"], wts["bg"], wts["qng"], wts["kvng"],
      wts["waT"], wts["wbT"], wts["wk"], wts["wvT"], tabs["cosqT"], tabs["sinqT"], tabs["cosk"], tabs["sink"])


def _gla_chunk(q, k, v, lg, s_t, reverse, need_out=True):
    C = q.shape[0]
    row = lax.broadcasted_iota(jnp.int32, (C, C), 0)
    col = lax.broadcasted_iota(jnp.int32, (C, C), 1)
    tri = jnp.where(col >= row if reverse else col <= row, 1.0, 0.0).astype(BF16)
    hi = lg.astype(BF16)
    r1 = lg - hi.astype(F32)
    mid = r1.astype(BF16)
    lo = (r1 - mid.astype(F32)).astype(BF16)
    b = _dot(tri, hi) + _dot(tri, mid) + _dot(tri, lo)
    b_end = b[0:1, :] if reverse else b[C - 1:C, :]
    kh = (k * jnp.exp(b_end - b)).astype(BF16)
    o = None
    if need_out:
        qt = (q * jnp.exp(b)).astype(BF16)
        kt = (k * jnp.exp(jnp.minimum(-b, 80.0))).astype(BF16)
        keep = col > row if reverse else col <= row
        lane_head = lax.broadcasted_iota(jnp.int32, (C, GK), 1) // GLA_DK
        vb = v.astype(BF16)
        outs = []
        for h in range(GLA_HEADS):
            qh = jnp.where(lane_head == h, qt, jnp.zeros_like(qt))
            a = jnp.where(keep, _dot_nt(qh, kt), 0.0).astype(BF16)
            outs.append(_dot(a, vb[:, h * GLA_DV:(h + 1) * GLA_DV]))
        o = jnp.concatenate(outs, axis=1) + _dot_nt(qt, s_t.astype(BF16))
    upd = _dot_tn(v.astype(BF16), kh)
    srow = lax.broadcasted_iota(jnp.int32, (GV, GK), 0) // GLA_DV
    scol = lax.broadcasted_iota(jnp.int32, (GV, GK), 1) // GLA_DK
    s_new = s_t * jnp.exp(b_end) + jnp.where(srow == scol, upd, 0.0)
    return o, s_new


def _gla_kernel(qkv_f_ref, lg_f_ref, qkv_b_ref, lg_b_ref, mqkv_ref, mlg_ref,
                of_ref, ob_ref, sf_ref, sb_ref, *, nch):
    C = GLA_CHUNK

    @pl.when(pl.program_id(1) == 0)
    def _():
        m = mqkv_ref[...]
        _, s0 = _gla_chunk(m[:, C_Q:C_K], m[:, C_K:C_V], m[:, C_V:C_G], mlg_ref[...],
                           jnp.zeros((GV, GK), F32), reverse=False, need_out=False)
        sf_ref[...] = s0
        sb_ref[...] = jnp.zeros((GV, GK), F32)

    def body(c, carry):
        rf = pl.multiple_of(c * C, C)
        rb = pl.multiple_of((nch - 1 - c) * C, C)
        o, s = _gla_chunk(qkv_f_ref[pl.ds(rf, C), C_Q:C_K], qkv_f_ref[pl.ds(rf, C), C_K:C_V],
                          qkv_f_ref[pl.ds(rf, C), C_V:C_G], lg_f_ref[pl.ds(rf, C), :],
                          sf_ref[...], reverse=False)
        of_ref[pl.ds(rf, C), :] = o
        sf_ref[...] = s
        o, s = _gla_chunk(qkv_b_ref[pl.ds(rb, C), C_Q:C_K], qkv_b_ref[pl.ds(rb, C), C_K:C_V],
                          qkv_b_ref[pl.ds(rb, C), C_V:C_G], lg_b_ref[pl.ds(rb, C), :],
                          sb_ref[...], reverse=True)
        ob_ref[pl.ds(rb, C), :] = o
        sb_ref[...] = s
        return carry

    lax.fori_loop(0, nch, body, 0)


def _gla(qkv, lg, meta_qkv, meta_lg, rows):
    B, S, _ = qkv.shape
    n = S // rows
    in_specs = [
        pl.BlockSpec((None, rows, C_G), lambda b, i: (b, i, 0)),
        pl.BlockSpec((None, rows, GK), lambda b, i: (b, i, 0)),
        pl.BlockSpec((None, rows, C_G), lambda b, i: (b, n - 1 - i, 0)),
        pl.BlockSpec((None, rows, GK), lambda b, i: (b, n - 1 - i, 1)),
        pl.BlockSpec((GLA_CHUNK, C_G), lambda b, i: (0, 0)),
        pl.BlockSpec((GLA_CHUNK, GK), lambda b, i: (0, 0)),
    ]
    out_specs = [
        pl.BlockSpec((None, rows, GV), lambda b, i: (b, i, 0)),
        pl.BlockSpec((None, rows, GV), lambda b, i: (b, n - 1 - i, 0)),
    ]
    out_shape = [jax.ShapeDtypeStruct((B, S, GV), F32)] * 2
    return pl.pallas_call(
        functools.partial(_gla_kernel, nch=rows // GLA_CHUNK),
        grid=(B, n), in_specs=in_specs, out_specs=out_specs, out_shape=out_shape,
        scratch_shapes=[pltpu.VMEM((GV, GK), F32), pltpu.VMEM((GV, GK), F32)],
        compiler_params=pltpu.CompilerParams(
            dimension_semantics=("parallel", "arbitrary"), vmem_limit_bytes=VMEM_LIMIT),
        name="gla",
    )(qkv, lg, qkv, lg, meta_qkv, meta_lg)


def _mla_kernel(qT_ref, k_ref, vT_ref, km_ref, vTm_ref, o_ref, *, nk, tk):
    tq = qT_ref.shape[-1]
    res = []
    for hh in range(2):
        ksl = slice(hh * HEAD_PAD, (hh + 1) * HEAD_PAD)
        vsl = slice(hh * MLA_DV, (hh + 1) * MLA_DV)
        qT = qT_ref[ksl, :]
        s = _dot(km_ref[:, ksl], qT)
        valid = lax.broadcasted_iota(jnp.int32, s.shape, 0) < N_META
        s = jnp.where(valid, s, NEG_BIG)
        m = jnp.max(s, axis=0, keepdims=True)
        p = jnp.exp2(s - m)
        l = jnp.sum(p, axis=0, keepdims=True)
        acc = _dot(vTm_ref[vsl, :], p.astype(BF16))

        def body(j, carry, ksl=ksl, vsl=vsl, qT=qT):
            m, l, acc = carry
            r0 = pl.multiple_of(j * tk, tk)
            s = _dot(k_ref[pl.ds(r0, tk), ksl], qT)
            m_new = jnp.maximum(m, jnp.max(s, axis=0, keepdims=True))
            alpha = jnp.exp2(m - m_new)
            p = jnp.exp2(s - m_new)
            l = alpha * l + jnp.sum(p, axis=0, keepdims=True)
            acc = alpha * acc + _dot(vT_ref[j, vsl, :], p.astype(BF16))
            return m_new, l, acc

        m, l, acc = lax.fori_loop(0, nk, body, (m, l, acc))
        res.append(acc * (1.0 / l))
    out = jnp.concatenate(res, axis=0)
    o_ref[...] = out.T.astype(o_ref.dtype)


def _mla(qT, k, vT, k_meta, vT_meta):
    B, n, _, t = qT.shape
    S = k.shape[1]
    npair = MLA_HEADS // 2
    in_specs = [
        pl.BlockSpec((None, None, 2 * HEAD_PAD, t), lambda b, hp, i: (b, i, hp, 0)),
        pl.BlockSpec((None, S, 2 * HEAD_PAD), lambda b, hp, i: (b, 0, hp)),
        pl.BlockSpec((None, n, 2 * MLA_DV, t), lambda b, hp, i: (b, 0, hp, 0)),
        pl.BlockSpec((LANE, 2 * HEAD_PAD), lambda b, hp, i: (0, hp)),
        pl.BlockSpec((2 * MLA_DV, LANE), lambda b, hp, i: (hp, 0)),
    ]
    out_specs = pl.BlockSpec((None, t, 2 * MLA_DV), lambda b, hp, i: (b, i, hp))
    return pl.pallas_call(
        functools.partial(_mla_kernel, nk=n, tk=t),
        grid=(B, npair, n), in_specs=in_specs, out_specs=out_specs,
        out_shape=jax.ShapeDtypeStruct((B, S, MV), BF16),
        compiler_params=pltpu.CompilerParams(
            dimension_semantics=("parallel", "parallel", "parallel"), vmem_limit_bytes=VMEM_LIMIT),
        name="mla",
    )(qT, k, vT, k_meta, vT_meta)


def _out_proj_kernel(x_ref, of_ref, ob_ref, g_ref, mla_ref, lng_ref, lnb_ref, gng_ref,
                     wog_ref, wom_ref, l1g_ref, l1b_ref, rw_ref, rb_ref,
                     h1_ref, h1b_ref, idx_ref, gate_ref):
    h0 = _layer_norm(x_ref[...], lng_ref[...], lnb_ref[...])
    o = of_ref[...] + ob_ref[...]
    gng = gng_ref[...]
    parts = []
    for h in range(GLA_HEADS):
        sl = slice(h * GLA_DV, (h + 1) * GLA_DV)
        parts.append(_rms_norm(o[:, sl], gng[:, sl]))
    g = g_ref[...].astype(F32)
    gla = jnp.concatenate(parts, axis=1) * (g * (1.0 / (1.0 + jnp.exp(-g))))
    mix = _dot(gla.astype(BF16), wog_ref[...]) + _dot(mla_ref[...], wom_ref[...])
    h1 = _layer_norm(ALPHA * h0 + mix, l1g_ref[...], l1b_ref[...])
    h1_ref[...] = h1
    h1b = h1.astype(BF16)
    h1b_ref[...] = h1b
    logits = _dot(h1b, rw_ref[...]) + rb_ref[...]
    lane = lax.broadcasted_iota(jnp.int32, logits.shape, 1)
    vals = jnp.where(lane < N_EXPERTS, logits, -jnp.inf)
    idx_out = jnp.zeros(logits.shape, jnp.int32)
    top = []
    for kk in range(TOP_K):
        mx = jnp.max(vals, axis=-1, keepdims=True)
        idx = jnp.min(jnp.where(vals == mx, lane, LANE), axis=-1, keepdims=True)
        top.append(mx)
        idx_out = jnp.where(lane == kk, idx, idx_out)
        vals = jnp.where(lane == idx, -jnp.inf, vals)
    es = [jnp.exp(t - top[0]) for t in top]
    inv = 1.0 / (es[0] + es[1] + es[2] + es[3])
    gate_out = jnp.zeros(logits.shape, F32)
    for kk in range(TOP_K):
        gate_out = jnp.where(lane == kk, es[kk] * inv, gate_out)
    idx_ref[...] = idx_out
    gate_ref[...] = gate_out


def _out_proj(x, o_f, o_b, g, mla, wts, tm):
    B, S, D = x.shape
    n = S // tm
    const = lambda shape: pl.BlockSpec(shape, lambda b, i: (0,) * len(shape))
    row = lambda w: pl.BlockSpec((None, tm, w), lambda b, i: (b, i, 0))
    in_specs = [row(D), row(GV), row(GV), row(GV), row(MV),
                const((1, D)), const((1, D)), const((1, GV)),
                const((GV, D)), const((MV, D)), const((1, D)), const((1, D)),
                const((D, LANE)), const((1, LANE))]
    out_shape = [jax.ShapeDtypeStruct((B, S, D), F32), jax.ShapeDtypeStruct((B, S, D), BF16),
                 jax.ShapeDtypeStruct((B, S, LANE), jnp.int32), jax.ShapeDtypeStruct((B, S, LANE), F32)]
    out_specs = [row(D), row(D), row(LANE), row(LANE)]
    return pl.pallas_call(
        _out_proj_kernel, grid=(B, n), in_specs=in_specs, out_specs=out_specs, out_shape=out_shape,
        compiler_params=pltpu.CompilerParams(
            dimension_semantics=("parallel", "parallel"), vmem_limit_bytes=VMEM_LIMIT),
        name="out_proj",
    )(x, o_f, o_b, g, mla, wts["ln_in_g"], wts["ln_in_b"], wts["gla_norm_g"], wts["wo_gla"], wts["wo_mla"],
      wts["ln1_g"], wts["ln1_b"], wts["router_w"], wts["router_b"])


def _moe_kernel(be_ref, nused_ref, xs_ref, sg_ref, wgu_ref, bgu_ref, wd_ref, bd_ref, y_ref,
                wgu_bf, wd_bf):
    i = pl.program_id(0)
    e = be_ref[i]
    e_prev = be_ref[jnp.maximum(i - 1, 0)]
    used = i < nused_ref[0]

    @pl.when(jnp.logical_and(used, jnp.logical_or(i == 0, e != e_prev)))
    def _():
        wgu_bf[...] = wgu_ref[...].astype(BF16)
        wd_bf[...] = wd_ref[...].astype(BF16)

    @pl.when(used)
    def _():
        gu = _dot(xs_ref[...], wgu_bf[...]) + bgu_ref[...]
        gate = jnp.minimum(gu[:, :D_FF], SWIGLU_LIMIT)
        up = jnp.clip(gu[:, D_FF:], -SWIGLU_LIMIT, SWIGLU_LIMIT)
        act = (up + 1.0) * gate * (1.0 / (1.0 + jnp.exp(-SWIGLU_ALPHA * gate)))
        y = _dot(act.astype(BF16), wd_bf[...]) + bd_ref[...]
        y_ref[...] = y * sg_ref[...]

    @pl.when(jnp.logical_not(used))
    def _():
        y_ref[...] = jnp.zeros(y_ref.shape, y_ref.dtype)


def _moe_ffn(xs, slot_gate, block_e, n_used, wts):
    slots, D = xs.shape
    bm = MOE_BLOCK
    n_blocks = slots // bm
    grid_spec = pltpu.PrefetchScalarGridSpec(
        num_scalar_prefetch=2, grid=(n_blocks,),
        in_specs=[
            pl.BlockSpec((bm, D), lambda i, be, nu: (i, 0)),
            pl.BlockSpec((bm, 1), lambda i, be, nu: (i, 0)),
            pl.BlockSpec((None, D, 2 * D_FF), lambda i, be, nu: (be[i], 0, 0)),
            pl.BlockSpec((None, 1, 2 * D_FF), lambda i, be, nu: (be[i], 0, 0)),
            pl.BlockSpec((None, D_FF, D), lambda i, be, nu: (be[i], 0, 0)),
            pl.BlockSpec((None, 1, D), lambda i, be, nu: (be[i], 0, 0)),
        ],
        out_specs=pl.BlockSpec((bm, D), lambda i, be, nu: (i, 0)),
        scratch_shapes=[pltpu.VMEM((D, 2 * D_FF), BF16), pltpu.VMEM((D_FF, D), BF16)],
    )
    return pl.pallas_call(
        _moe_kernel, grid_spec=grid_spec, out_shape=jax.ShapeDtypeStruct((slots, D), F32),
        compiler_params=pltpu.CompilerParams(
            dimension_semantics=("arbitrary",), vmem_limit_bytes=VMEM_LIMIT),
        name="moe_ffn",
    )(block_e, n_used, xs, slot_gate, wts["w_gu"], wts["b_gu"], wts["w_down"], wts["b_down"])


def _final_ln_kernel(h1_ref, y4_ref, g_ref, b_ref, o_ref):
    ffn = (y4_ref[0] + y4_ref[1]) + (y4_ref[2] + y4_ref[3])
    o_ref[...] = _layer_norm(ALPHA * h1_ref[...] + ffn, g_ref[...], b_ref[...])


def _final_ln(h1, y4, g, b, tm):
    T, D = h1.shape
    return pl.pallas_call(
        _final_ln_kernel, grid=(T // tm,),
        in_specs=[pl.BlockSpec((tm, D), lambda i: (i, 0)),
                  pl.BlockSpec((TOP_K, tm, D), lambda i: (0, i, 0)),
                  pl.BlockSpec((1, D), lambda i: (0, 0)), pl.BlockSpec((1, D), lambda i: (0, 0))],
        out_specs=pl.BlockSpec((tm, D), lambda i: (i, 0)),
        out_shape=jax.ShapeDtypeStruct((T, D), F32),
        compiler_params=pltpu.CompilerParams(
            dimension_semantics=("parallel",), vmem_limit_bytes=VMEM_LIMIT),
        name="final_ln",
    )(h1, y4, g, b)


def _prep_weights(p):
    w_in = p["w_in"][0]
    sizes = (GK, GK, GV, GLA_RANK, GLA_RANK, GV, MLA_QR, MLA_KVR, MLA_ROPE)
    offs = [0]
    for s in sizes:
        offs.append(offs[-1] + s)
    wq, wk_, wv, wlf, wlb, wg_, wcq, wckv, wkr = [w_in[:, offs[j]:offs[j + 1]] for j in range(9)]
    D = w_in.shape[0]
    half = MLA_ROPE // 2
    z = lambda n: jnp.zeros((D, n), F32)
    kra = jnp.concatenate([z(MLA_NOPE), wkr, z(HEAD_PAD - MLA_NOPE - MLA_ROPE)], axis=1)
    krb = jnp.concatenate([z(MLA_NOPE), -wkr[:, half:], wkr[:, :half], z(HEAD_PAD - MLA_NOPE - MLA_ROPE)], axis=1)
    w1 = jnp.concatenate([wq * (GLA_DK ** -0.5), wk_, wv, wg_, wcq, wckv, kra, krb, wlf, wlb,
                          z(C_END - C_LR - 2 * GLA_RANK)], axis=1).astype(BF16)
    wg = jnp.zeros((LANE, 2 * GK), F32)
    wg = wg.at[0:GLA_RANK, 0:GK].set(p["gla_gate_w_fwd"][0])
    wg = wg.at[GLA_RANK:2 * GLA_RANK, GK:].set(p["gla_gate_w_bwd"][0]).astype(BF16)
    bg = jnp.concatenate([p["gla_gate_b_fwd"][0], p["gla_gate_b_bwd"][0]])[None, :]
    c = (MLA_NOPE + MLA_ROPE) ** -0.5 * math.log2(math.e)
    wuq = p["mla_w_uq"][0].reshape(MLA_QR, MLA_HEADS, MLA_NOPE + MLA_ROPE) * c
    nope, rope = wuq[:, :, :MLA_NOPE], wuq[:, :, MLA_NOPE:]
    zq = jnp.zeros((MLA_QR, MLA_HEADS, HEAD_PAD - MLA_NOPE - MLA_ROPE), F32)
    wa = jnp.concatenate([nope, rope, zq], axis=2).reshape(MLA_QR, MQ)
    wb = jnp.concatenate([jnp.zeros_like(nope), -rope[:, :, half:], rope[:, :, :half], zq], axis=2).reshape(MLA_QR, MQ)
    wukv = p["mla_w_ukv"][0].reshape(MLA_KVR, MLA_HEADS, MLA_NOPE + MLA_DV)
    wk = jnp.concatenate([wukv[:, :, :MLA_NOPE], jnp.zeros((MLA_KVR, MLA_HEADS, HEAD_PAD - MLA_NOPE), F32)],
                         axis=2).reshape(MLA_KVR, MQ)
    wv_ = wukv[:, :, MLA_NOPE:].reshape(MLA_KVR, MV)
    w_out = p["w_out"][0]
    rw = jnp.zeros((D, LANE), F32).at[:, :N_EXPERTS].set(p["router_w"][0]).astype(BF16)
    rb = jnp.zeros((1, LANE), F32).at[0, :N_EXPERTS].set(p["router_b"][0])
    row = lambda a: a.reshape(1, -1)
    return dict(
        ln_in_g=row(p["ln_in_g"]), ln_in_b=row(p["ln_in_b"]), w1=w1, wg=wg, bg=bg,
        qng=row(p["mla_q_norm_g"][0]), kvng=row(p["mla_kv_norm_g"][0]),
        waT=wa.T.astype(BF16), wbT=wb.T.astype(BF16), wk=wk.astype(BF16), wvT=wv_.T.astype(BF16),
        gla_norm_g=row(p["gla_norm_g"][0]), wo_gla=w_out[:GV].astype(BF16), wo_mla=w_out[GV:].astype(BF16),
        ln1_g=row(p["ln1_g"][0]), ln1_b=row(p["ln1_b"][0]), router_w=rw, router_b=rb,
        w_gu=p["expert_w_gu"][0], b_gu=p["expert_b_gu"][0][:, None, :],
        w_down=p["expert_w_down"][0], b_down=p["expert_b_down"][0][:, None, :],
        ln2_g=row(p["ln2_g"][0]), ln2_b=row(p["ln2_b"][0]),
    )


def _rope_tables(start, n):
    half = MLA_ROPE // 2
    inv = ROPE_BASE ** (-jnp.arange(0, MLA_ROPE, 2, dtype=F32) / MLA_ROPE)
    ang = (jnp.arange(n, dtype=F32) + float(start))[:, None] * inv[None, :]
    cos, sin = jnp.cos(ang), jnp.sin(ang)
    cos2 = jnp.concatenate([cos, cos], axis=1)
    sin2 = jnp.concatenate([sin, sin], axis=1)
    tail = jnp.zeros((n, HEAD_PAD - MLA_NOPE - MLA_ROPE), F32)
    cosq = jnp.concatenate([jnp.ones((n, MLA_NOPE), F32), cos2, tail], axis=1)
    rot = jnp.concatenate([jnp.zeros((n, MLA_NOPE), F32), sin2, tail], axis=1)
    cosk = jnp.concatenate([jnp.zeros((n, MLA_NOPE), F32), cos2, tail], axis=1)
    return dict(cosqT=cosq.T, sinqT=rot.T, cosk=cosk, sink=rot)


def _route(top_idx, gates, bm):
    T = top_idx.shape[0]
    n_assign = T * TOP_K
    flat_e = top_idx.reshape(-1)
    onehot = (flat_e[:, None] == jnp.arange(N_EXPERTS, dtype=jnp.int32)[None, :]).astype(jnp.int32)
    csum = jnp.cumsum(onehot, axis=0)
    rank = jnp.take_along_axis(csum, flat_e[:, None], axis=1)[:, 0] - 1
    counts = csum[-1]
    pcounts = (counts + bm - 1) // bm * bm
    pends = jnp.cumsum(pcounts)
    pstarts = pends - pcounts
    dest = pstarts[flat_e] + rank
    n_blocks = -(-n_assign // bm) + N_EXPERTS
    slots = n_blocks * bm
    slot_tok = jnp.zeros((slots,), jnp.int32).at[dest].set(jnp.arange(n_assign, dtype=jnp.int32) // TOP_K)
    slot_gate = jnp.zeros((slots,), F32).at[dest].set(gates.reshape(-1))
    block_e = jnp.minimum(
        jnp.searchsorted(pends, jnp.arange(n_blocks, dtype=pends.dtype) * bm, side="right"),
        N_EXPERTS - 1).astype(jnp.int32)
    n_used = (pends[-1:] // bm).astype(jnp.int32)
    return slot_tok, slot_gate, block_e, n_used, dest


def _trunk(x, wts, meta):
    B, S, D = x.shape
    tm = min(512, S)
    qkv, lg, g, qT, k, vT = _in_proj(x, wts, _rope_tables(N_META, S), tm)
    o_f, o_b = _gla(qkv, lg, meta["qkv"], meta["lg"], tm)
    mla = _mla(qT, k, vT, meta["k"], meta["vT"])
    h1, h1b, idx, gate = _out_proj(x, o_f, o_b, g, mla, wts, tm)
    T = B * S
    h1 = h1.reshape(T, D)
    h1b = h1b.reshape(T, D)
    top_idx = idx.reshape(T, LANE)[:, :TOP_K]
    gates = gate.reshape(T, LANE)[:, :TOP_K]
    slot_tok, slot_gate, block_e, n_used, dest = _route(top_idx, gates, MOE_BLOCK)
    xs = jnp.take(h1b, slot_tok, axis=0)
    y = _moe_ffn(xs, slot_gate[:, None], block_e, n_used, wts)
    y4 = jnp.take(y, dest.reshape(T, TOP_K).T, axis=0)
    out = _final_ln(h1, y4, wts["ln2_g"], wts["ln2_b"], tm)
    return out.reshape(B, S, D)


def _meta_prep(meta_tokens, wts):
    xm = jnp.zeros((1, LANE, D_MODEL), F32).at[0, :N_META].set(meta_tokens)
    qkv, lg, _, _, k, vT = _in_proj(xm, wts, _rope_tables(0, LANE), LANE)
    front = GLA_CHUNK - N_META
    mq = jnp.zeros((GLA_CHUNK, C_G), F32).at[front:].set(qkv[0, :N_META])
    ml = jnp.zeros((GLA_CHUNK, GK), F32).at[front:].set(lg[0, :N_META, :GK])
    return dict(qkv=mq, lg=ml, k=k[0], vT=vT[0, 0])


def kernel(x_prompt, x_sample, meta_tokens, ln_in_g, ln_in_b, w_in, gla_gate_w_fwd, gla_gate_b_fwd,
           gla_gate_w_bwd, gla_gate_b_bwd, gla_norm_g, mla_q_norm_g, mla_w_uq, mla_kv_norm_g, mla_w_ukv,
           w_out, ln1_g, ln1_b, router_w, router_b, expert_w_gu, expert_b_gu, expert_w_down, expert_b_down,
           ln2_g, ln2_b):
    wts = _prep_weights(dict(
        ln_in_g=ln_in_g, ln_in_b=ln_in_b, w_in=w_in, gla_gate_w_fwd=gla_gate_w_fwd,
        gla_gate_b_fwd=gla_gate_b_fwd, gla_gate_w_bwd=gla_gate_w_bwd, gla_gate_b_bwd=gla_gate_b_bwd,
        gla_norm_g=gla_norm_g, mla_q_norm_g=mla_q_norm_g, mla_w_uq=mla_w_uq, mla_kv_norm_g=mla_kv_norm_g,
        mla_w_ukv=mla_w_ukv, w_out=w_out, ln1_g=ln1_g, ln1_b=ln1_b, router_w=router_w, router_b=router_b,
        expert_w_gu=expert_w_gu, expert_b_gu=expert_b_gu, expert_w_down=expert_w_down,
        expert_b_down=expert_b_down, ln2_g=ln2_g, ln2_b=ln2_b))
    meta = _meta_prep(meta_tokens, wts)
    return (_trunk(x_prompt, wts, meta), _trunk(x_sample, wts, meta))
```

```python
import functools
import math

import jax
import jax.numpy as jnp
from jax import lax
from jax.experimental import pallas as pl
from jax.experimental.pallas import tpu as pltpu

D_MODEL = 1024
N_META = 16
GLA_HEADS, GLA_DK, GLA_DV, GLA_RANK = 4, 64, 128, 16
GLA_TAU = 16.0
GLA_CHUNK = 64
MLA_HEADS, MLA_QR, MLA_KVR, MLA_NOPE, MLA_ROPE, MLA_DV = 8, 256, 128, 64, 32, 64
ROPE_BASE = 10000.0
N_EXPERTS, TOP_K, D_FF = 32, 4, 1024
SWIGLU_LIMIT, SWIGLU_ALPHA = 7.0, 1.702
DEPTH = 1
ALPHA = (2.0 * DEPTH) ** 0.25
LN_EPS, RMS_EPS = 1e-5, 1e-6

LANE = 128
HEAD_PAD = 128
GK = GLA_HEADS * GLA_DK
GV = GLA_HEADS * GLA_DV
MV = MLA_HEADS * MLA_DV
VT_ROWS = MLA_DV + 16
MVT = MLA_HEADS * VT_ROWS
MQ = MLA_HEADS * HEAD_PAD
C_Q, C_K, C_V, C_G, C_CQ, C_CKV, C_KRA, C_KRB, C_LR, C_END = (
    0, 256, 512, 1024, 1536, 1792, 1920, 2048, 2176, 2304)
MOE_BLOCK = 256
NEG_BIG = -1e30
VMEM_LIMIT = 56 * 1024 * 1024

BF16 = jnp.bfloat16
F32 = jnp.float32


def _dot(a, b):
    return jnp.dot(a, b, preferred_element_type=F32)


def _dot_nt(a, b):
    return lax.dot_general(a, b, (((1,), (1,)), ((), ())), preferred_element_type=F32)


def _dot_tn(a, b):
    return lax.dot_general(a, b, (((0,), (0,)), ((), ())), preferred_element_type=F32)


def _layer_norm(x, g, b):
    mu = jnp.mean(x, axis=-1, keepdims=True)
    xc = x - mu
    var = jnp.mean(xc * xc, axis=-1, keepdims=True)
    return xc * lax.rsqrt(var + LN_EPS) * g + b


def _rms_norm(x, g):
    ms = jnp.mean(x * x, axis=-1, keepdims=True)
    return x * lax.rsqrt(ms + RMS_EPS) * g


def _log_sigmoid(z):
    return jnp.minimum(z, 0.0) - jnp.log(1.0 + jnp.exp(-jnp.abs(z)))


def _in_proj_kernel(x_ref, lng_ref, lnb_ref, w1_ref, wg_ref, bg_ref, qng_ref, kvng_ref,
                    waT_ref, wbT_ref, wk_ref, wvT_ref, cosqT_ref, sinqT_ref, cosk_ref, sink_ref,
                    qkv_ref, lg_ref, g_ref, qT_ref, k_ref, vT_ref):
    xn = _layer_norm(x_ref[...], lng_ref[...], lnb_ref[...])
    p = _dot(xn.astype(BF16), w1_ref[...])
    qkv_ref[...] = p[:, C_Q:C_G]
    g_ref[...] = p[:, C_G:C_CQ].astype(BF16)
    z = _dot(p[:, C_LR:C_END].astype(BF16), wg_ref[...]) + bg_ref[...]
    lg_ref[...] = _log_sigmoid(z) * (1.0 / GLA_TAU)
    cqn = _rms_norm(p[:, C_CQ:C_CKV], qng_ref[...]).astype(BF16)
    qaT = _dot_nt(waT_ref[...], cqn)
    qbT = _dot_nt(wbT_ref[...], cqn)
    cosq, sinq = cosqT_ref[...], sinqT_ref[...]
    for h in range(MLA_HEADS):
        sl = slice(h * HEAD_PAD, (h + 1) * HEAD_PAD)
        qT_ref[sl, :] = (qaT[sl, :] * cosq + qbT[sl, :] * sinq).astype(BF16)
    ckvn = _rms_norm(p[:, C_CKV:C_KRA], kvng_ref[...]).astype(BF16)
    knp = _dot(ckvn, wk_ref[...])
    krope = p[:, C_KRA:C_KRB] * cosk_ref[...] + p[:, C_KRB:C_LR] * sink_ref[...]
    for h in range(MLA_HEADS):
        sl = slice(h * HEAD_PAD, (h + 1) * HEAD_PAD)
        k_ref[:, sl] = (knp[:, sl] + krope).astype(BF16)
    vT = _dot_nt(wvT_ref[...], ckvn).astype(BF16)
    tail_rows = VT_ROWS - MLA_DV
    ones_row = jnp.where(lax.broadcasted_iota(jnp.int32, (tail_rows, vT.shape[1]), 0) == 0, 1.0, 0.0).astype(BF16)
    for h in range(MLA_HEADS):
        vT_ref[h * VT_ROWS:h * VT_ROWS + MLA_DV, :] = vT[h * MLA_DV:(h + 1) * MLA_DV, :]
        vT_ref[h * VT_ROWS + MLA_DV:(h + 1) * VT_ROWS, :] = ones_row


def _in_proj(x, wts, tabs, tm):
    B, S, D = x.shape
    n = S // tm
    const = lambda shape: pl.BlockSpec(shape, lambda b, i: (0,) * len(shape))
    in_specs = [
        pl.BlockSpec((None, tm, D), lambda b, i: (b, i, 0)),
        const((1, D)), const((1, D)),
        const((D, C_END)), const((LANE, 2 * GK)), const((1, 2 * GK)),
        const((1, MLA_QR)), const((1, MLA_KVR)),
        const((MQ, MLA_QR)), const((MQ, MLA_QR)),
        const((MLA_KVR, MQ)), const((MV, MLA_KVR)),
        pl.BlockSpec((HEAD_PAD, tm), lambda b, i: (0, i)),
        pl.BlockSpec((HEAD_PAD, tm), lambda b, i: (0, i)),
        pl.BlockSpec((tm, HEAD_PAD), lambda b, i: (i, 0)),
        pl.BlockSpec((tm, HEAD_PAD), lambda b, i: (i, 0)),
    ]
    out_shape = [
        jax.ShapeDtypeStruct((B, S, C_G), F32),
        jax.ShapeDtypeStruct((B, S, 2 * GK), F32),
        jax.ShapeDtypeStruct((B, S, GV), BF16),
        jax.ShapeDtypeStruct((B, n, MQ, tm), BF16),
        jax.ShapeDtypeStruct((B, S, MQ), BF16),
        jax.ShapeDtypeStruct((B, n, MVT, tm), BF16),
    ]
    out_specs = [
        pl.BlockSpec((None, tm, C_G), lambda b, i: (b, i, 0)),
        pl.BlockSpec((None, tm, 2 * GK), lambda b, i: (b, i, 0)),
        pl.BlockSpec((None, tm, GV), lambda b, i: (b, i, 0)),
        pl.BlockSpec((None, None, MQ, tm), lambda b, i: (b, i, 0, 0)),
        pl.BlockSpec((None, tm, MQ), lambda b, i: (b, i, 0)),
        pl.BlockSpec((None, None, MVT, tm), lambda b, i: (b, i, 0, 0)),
    ]
    return pl.pallas_call(
        _in_proj_kernel, grid=(B, n), in_specs=in_specs, out_specs=out_specs, out_shape=out_shape,
        compiler_params=pltpu.CompilerParams(
            dimension_semantics=("parallel", "parallel"), vmem_limit_bytes=VMEM_LIMIT),
        name="in_proj",
    )(x, wts["ln_in_g"], wts["ln_in_b"], wts["w1"], wts["wg"], wts["bg"], wts["qng"], wts["kvng"],
      wts["waT"], wts["wbT"], wts["wk"], wts["wvT"], tabs["cosqT"], tabs["sinqT"], tabs["cosk"], tabs["sink"])


def _gla_chunk(q, k, v, lg, s_t, reverse, need_out=True):
    C = q.shape[0]
    row = lax.broadcasted_iota(jnp.int32, (C, C), 0)
    col = lax.broadcasted_iota(jnp.int32, (C, C), 1)
    tri = jnp.where(col >= row if reverse else col <= row, 1.0, 0.0).astype(BF16)
    hi = lg.astype(BF16)
    r1 = lg - hi.astype(F32)
    mid = r1.astype(BF16)
    lo = (r1 - mid.astype(F32)).astype(BF16)
    b = _dot(tri, hi) + _dot(tri, mid) + _dot(tri, lo)
    b_end = b[0:1, :] if reverse else b[C - 1:C, :]
    kh = (k * jnp.exp(b_end - b)).astype(BF16)
    o = None
    if need_out:
        qt = (q * jnp.exp(b)).astype(BF16)
        kt = (k * jnp.exp(jnp.minimum(-b, 80.0))).astype(BF16)
        keep = col > row if reverse else col <= row
        lane_head = lax.broadcasted_iota(jnp.int32, (C, GK), 1) // GLA_DK
        vb = v.astype(BF16)
        outs = []
        for h in range(GLA_HEADS):
            qh = jnp.where(lane_head == h, qt, jnp.zeros_like(qt))
            a = jnp.where(keep, _dot_nt(qh, kt), 0.0).astype(BF16)
            outs.append(_dot(a, vb[:, h * GLA_DV:(h + 1) * GLA_DV]))
        o = jnp.concatenate(outs, axis=1) + _dot_nt(qt, s_t.astype(BF16))
    upd = _dot_tn(v.astype(BF16), kh)
    srow = lax.broadcasted_iota(jnp.int32, (GV, GK), 0) // GLA_DV
    scol = lax.broadcasted_iota(jnp.int32, (GV, GK), 1) // GLA_DK
    s_new = s_t * jnp.exp(b_end) + jnp.where(srow == scol, upd, 0.0)
    return o, s_new


def _gla_kernel(qkv_f_ref, lg_f_ref, qkv_b_ref, lg_b_ref, mqkv_ref, mlg_ref,
                of_ref, ob_ref, sf_ref, sb_ref, *, nch):
    C = GLA_CHUNK

    @pl.when(pl.program_id(1) == 0)
    def _():
        m = mqkv_ref[...]
        _, s0 = _gla_chunk(m[:, C_Q:C_K], m[:, C_K:C_V], m[:, C_V:C_G], mlg_ref[...],
                           jnp.zeros((GV, GK), F32), reverse=False, need_out=False)
        sf_ref[...] = s0
        sb_ref[...] = jnp.zeros((GV, GK), F32)

    def body(c, carry):
        rf = pl.multiple_of(c * C, C)
        rb = pl.multiple_of((nch - 1 - c) * C, C)
        o, s = _gla_chunk(qkv_f_ref[pl.ds(rf, C), C_Q:C_K], qkv_f_ref[pl.ds(rf, C), C_K:C_V],
                          qkv_f_ref[pl.ds(rf, C), C_V:C_G], lg_f_ref[pl.ds(rf, C), :],
                          sf_ref[...], reverse=False)
        of_ref[pl.ds(rf, C), :] = o
        sf_ref[...] = s
        o, s = _gla_chunk(qkv_b_ref[pl.ds(rb, C), C_Q:C_K], qkv_b_ref[pl.ds(rb, C), C_K:C_V],
                          qkv_b_ref[pl.ds(rb, C), C_V:C_G], lg_b_ref[pl.ds(rb, C), :],
                          sb_ref[...], reverse=True)
        ob_ref[pl.ds(rb, C), :] = o
        sb_ref[...] = s
        return carry

    lax.fori_loop(0, nch, body, 0)


def _gla(qkv, lg, meta_qkv, meta_lg, rows):
    B, S, _ = qkv.shape
    n = S // rows
    in_specs = [
        pl.BlockSpec((None, rows, C_G), lambda b, i: (b, i, 0)),
        pl.BlockSpec((None, rows, GK), lambda b, i: (b, i, 0)),
        pl.BlockSpec((None, rows, C_G), lambda b, i: (b, n - 1 - i, 0)),
        pl.BlockSpec((None, rows, GK), lambda b, i: (b, n - 1 - i, 1)),
        pl.BlockSpec((GLA_CHUNK, C_G), lambda b, i: (0, 0)),
        pl.BlockSpec((GLA_CHUNK, GK), lambda b, i: (0, 0)),
    ]
    out_specs = [
        pl.BlockSpec((None, rows, GV), lambda b, i: (b, i, 0)),
        pl.BlockSpec((None, rows, GV), lambda b, i: (b, n - 1 - i, 0)),
    ]
    out_shape = [jax.ShapeDtypeStruct((B, S, GV), F32)] * 2
    return pl.pallas_call(
        functools.partial(_gla_kernel, nch=rows // GLA_CHUNK),
        grid=(B, n), in_specs=in_specs, out_specs=out_specs, out_shape=out_shape,
        scratch_shapes=[pltpu.VMEM((GV, GK), F32), pltpu.VMEM((GV, GK), F32)],
        compiler_params=pltpu.CompilerParams(
            dimension_semantics=("parallel", "arbitrary"), vmem_limit_bytes=VMEM_LIMIT),
        name="gla",
    )(qkv, lg, qkv, lg, meta_qkv, meta_lg)


def _mla_kernel(qT_ref, k_ref, vT_ref, km_ref, vTm_ref, o_ref, s_ref, m_ref, acc_ref, *, nk, tk):
    ksl = [slice(hh * HEAD_PAD, (hh + 1) * HEAD_PAD) for hh in range(2)]
    vsl = [slice(hh * VT_ROWS, (hh + 1) * VT_ROWS) for hh in range(2)]

    for hh in range(2):
        s = _dot(km_ref[:, ksl[hh]], qT_ref[ksl[hh], :])
        valid = lax.broadcasted_iota(jnp.int32, s.shape, 0) < N_META
        s = jnp.where(valid, s, NEG_BIG)
        m = jnp.max(s, axis=0, keepdims=True)
        m_ref[hh] = m
        acc_ref[hh] = _dot(vTm_ref[vsl[hh], :], jnp.exp2(s - m).astype(BF16))

    def scores(j, slot):
        r0 = pl.multiple_of(j * tk, tk)
        for hh in range(2):
            s_ref[slot, hh] = _dot(k_ref[pl.ds(r0, tk), ksl[hh]], qT_ref[ksl[hh], :])

    def consume(j, slot):
        for hh in range(2):
            s = s_ref[slot, hh]
            m = m_ref[hh]
            m_new = jnp.maximum(m, jnp.max(s, axis=0, keepdims=True))
            p = jnp.exp2(s - m_new).astype(BF16)
            acc_ref[hh] = jnp.exp2(m - m_new) * acc_ref[hh] + _dot(vT_ref[j, vsl[hh], :], p)
            m_ref[hh] = m_new

    scores(0, 0)

    def body(i, carry):
        scores(2 * i + 1, 1)
        consume(2 * i, 0)
        scores(jnp.minimum(2 * i + 2, nk - 1), 0)
        consume(2 * i + 1, 1)
        return carry

    lax.fori_loop(0, nk // 2, body, 0)
    res = []
    for hh in range(2):
        acc = acc_ref[hh]
        res.append(acc[:MLA_DV, :] * (1.0 / acc[MLA_DV:MLA_DV + 1, :]))
    out = jnp.concatenate(res, axis=0)
    o_ref[...] = out.T.astype(o_ref.dtype)


def _mla(qT, k, vT, k_meta, vT_meta):
    B, n, _, t = qT.shape
    S = k.shape[1]
    assert n % 2 == 0
    npair = MLA_HEADS // 2
    in_specs = [
        pl.BlockSpec((None, None, 2 * HEAD_PAD, t), lambda b, hp, i: (b, i, hp, 0)),
        pl.BlockSpec((None, S, 2 * HEAD_PAD), lambda b, hp, i: (b, 0, hp)),
        pl.BlockSpec((None, n, 2 * VT_ROWS, t), lambda b, hp, i: (b, 0, hp, 0)),
        pl.BlockSpec((LANE, 2 * HEAD_PAD), lambda b, hp, i: (0, hp)),
        pl.BlockSpec((2 * VT_ROWS, LANE), lambda b, hp, i: (hp, 0)),
    ]
    out_specs = pl.BlockSpec((None, t, 2 * MLA_DV), lambda b, hp, i: (b, i, hp))
    return pl.pallas_call(
        functools.partial(_mla_kernel, nk=n, tk=t),
        grid=(B, npair, n), in_specs=in_specs, out_specs=out_specs,
        out_shape=jax.ShapeDtypeStruct((B, S, MV), BF16),
        scratch_shapes=[pltpu.VMEM((2, 2, t, t), F32), pltpu.VMEM((2, 1, t), F32),
                        pltpu.VMEM((2, VT_ROWS, t), F32)],
        compiler_params=pltpu.CompilerParams(
            dimension_semantics=("parallel", "parallel", "parallel"), vmem_limit_bytes=VMEM_LIMIT),
        name="mla",
    )(qT, k, vT, k_meta, vT_meta)


def _out_proj_kernel(x_ref, of_ref, ob_ref, g_ref, mla_ref, lng_ref, lnb_ref, gng_ref,
                     wog_ref, wom_ref, l1g_ref, l1b_ref, rw_ref, rb_ref,
                     h1_ref, h1b_ref, route_ref, gate_ref, cnt_ref, run_ref):
    @pl.when(jnp.logical_and(pl.program_id(0) == 0, pl.program_id(1) == 0))
    def _():
        run_ref[...] = jnp.zeros(run_ref.shape, run_ref.dtype)

    h0 = _layer_norm(x_ref[...], lng_ref[...], lnb_ref[...])
    o = of_ref[...] + ob_ref[...]
    gng = gng_ref[...]
    parts = []
    for h in range(GLA_HEADS):
        sl = slice(h * GLA_DV, (h + 1) * GLA_DV)
        parts.append(_rms_norm(o[:, sl], gng[:, sl]))
    g = g_ref[...].astype(F32)
    gla = jnp.concatenate(parts, axis=1) * (g * (1.0 / (1.0 + jnp.exp(-g))))
    mix = _dot(gla.astype(BF16), wog_ref[...]) + _dot(mla_ref[...], wom_ref[...])
    h1 = _layer_norm(ALPHA * h0 + mix, l1g_ref[...], l1b_ref[...])
    h1_ref[...] = h1
    h1b = h1.astype(BF16)
    h1b_ref[...] = h1b
    logits = _dot(h1b, rw_ref[...]) + rb_ref[...]
    lane = lax.broadcasted_iota(jnp.int32, logits.shape, 1)
    vals = jnp.where(lane < N_EXPERTS, logits, -jnp.inf)
    route = jnp.zeros(logits.shape, jnp.int32)
    onehot = jnp.zeros(logits.shape, F32)
    top, sel = [], []
    for kk in range(TOP_K):
        mx = jnp.max(vals, axis=-1, keepdims=True)
        idx = jnp.min(jnp.where(vals == mx, lane, LANE), axis=-1, keepdims=True)
        top.append(mx)
        hit = lane == idx
        sel.append(hit)
        route = jnp.where(lane == kk, idx, route)
        onehot = jnp.where(hit, 1.0, onehot)
        vals = jnp.where(hit, -jnp.inf, vals)
    tm = logits.shape[0]
    earlier = (lax.broadcasted_iota(jnp.int32, (tm, tm), 1) < lax.broadcasted_iota(jnp.int32, (tm, tm), 0))
    before = _dot(jnp.where(earlier, 1.0, 0.0).astype(BF16), onehot.astype(BF16)) + run_ref[...]
    for kk in range(TOP_K):
        rk = jnp.sum(jnp.where(sel[kk], before, 0.0), axis=-1, keepdims=True)
        route = jnp.where(lane == TOP_K + kk, rk.astype(jnp.int32), route)
    run_ref[...] = run_ref[...] + jnp.sum(onehot, axis=0, keepdims=True)
    cnt_ref[...] = run_ref[...].astype(jnp.int32)
    es = [jnp.exp(t - top[0]) for t in top]
    inv = 1.0 / (es[0] + es[1] + es[2] + es[3])
    gate_out = jnp.zeros(logits.shape, F32)
    for kk in range(TOP_K):
        gate_out = jnp.where(lane == kk, es[kk] * inv, gate_out)
    route_ref[...] = route
    gate_ref[...] = gate_out


def _out_proj(x, o_f, o_b, g, mla, wts, tm):
    B, S, D = x.shape
    n = S // tm
    const = lambda shape: pl.BlockSpec(shape, lambda b, i: (0,) * len(shape))
    row = lambda w: pl.BlockSpec((None, tm, w), lambda b, i: (b, i, 0))
    in_specs = [row(D), row(GV), row(GV), row(GV), row(MV),
                const((1, D)), const((1, D)), const((1, GV)),
                const((GV, D)), const((MV, D)), const((1, D)), const((1, D)),
                const((D, LANE)), const((1, LANE))]
    out_shape = [jax.ShapeDtypeStruct((B, S, D), F32), jax.ShapeDtypeStruct((B, S, D), BF16),
                 jax.ShapeDtypeStruct((B, S, LANE), jnp.int32), jax.ShapeDtypeStruct((B, S, LANE), F32),
                 jax.ShapeDtypeStruct((1, LANE), jnp.int32)]
    out_specs = [row(D), row(D), row(LANE), row(LANE), const((1, LANE))]
    return pl.pallas_call(
        _out_proj_kernel, grid=(B, n), in_specs=in_specs, out_specs=out_specs, out_shape=out_shape,
        scratch_shapes=[pltpu.VMEM((1, LANE), F32)],
        compiler_params=pltpu.CompilerParams(
            dimension_semantics=("arbitrary", "arbitrary"), vmem_limit_bytes=VMEM_LIMIT),
        name="out_proj",
    )(x, o_f, o_b, g, mla, wts["ln_in_g"], wts["ln_in_b"], wts["gla_norm_g"], wts["wo_gla"], wts["wo_mla"],
      wts["ln1_g"], wts["ln1_b"], wts["router_w"], wts["router_b"])


def _moe_kernel(be_ref, nused_ref, xs_ref, wgu_ref, bgu_ref, wd_ref, bd_ref, y_ref,
                wgu_bf, wd_bf):
    i = pl.program_id(0)
    e = be_ref[i]
    e_prev = be_ref[jnp.maximum(i - 1, 0)]
    used = i < nused_ref[0]

    @pl.when(jnp.logical_and(used, jnp.logical_or(i == 0, e != e_prev)))
    def _():
        wgu_bf[...] = wgu_ref[...].astype(BF16)
        wd_bf[...] = wd_ref[...].astype(BF16)

    @pl.when(used)
    def _():
        gu = _dot(xs_ref[...], wgu_bf[...]) + bgu_ref[...]
        gate = jnp.minimum(gu[:, :D_FF], SWIGLU_LIMIT)
        up = jnp.clip(gu[:, D_FF:], -SWIGLU_LIMIT, SWIGLU_LIMIT)
        act = (up + 1.0) * gate * (1.0 / (1.0 + jnp.exp(-SWIGLU_ALPHA * gate)))
        y = _dot(act.astype(BF16), wd_bf[...]) + bd_ref[...]
        y_ref[...] = y.astype(y_ref.dtype)

    @pl.when(jnp.logical_not(used))
    def _():
        y_ref[...] = jnp.zeros(y_ref.shape, y_ref.dtype)


def _moe_ffn(xs, block_e, n_used, wts):
    slots, D = xs.shape
    bm = MOE_BLOCK
    n_blocks = slots // bm
    grid_spec = pltpu.PrefetchScalarGridSpec(
        num_scalar_prefetch=2, grid=(n_blocks,),
        in_specs=[
            pl.BlockSpec((bm, D), lambda i, be, nu: (i, 0)),
            pl.BlockSpec((None, D, 2 * D_FF), lambda i, be, nu: (be[i], 0, 0)),
            pl.BlockSpec((None, 1, 2 * D_FF), lambda i, be, nu: (be[i], 0, 0)),
            pl.BlockSpec((None, D_FF, D), lambda i, be, nu: (be[i], 0, 0)),
            pl.BlockSpec((None, 1, D), lambda i, be, nu: (be[i], 0, 0)),
        ],
        out_specs=pl.BlockSpec((bm, D), lambda i, be, nu: (i, 0)),
        scratch_shapes=[pltpu.VMEM((D, 2 * D_FF), BF16), pltpu.VMEM((D_FF, D), BF16)],
    )
    return pl.pallas_call(
        _moe_kernel, grid_spec=grid_spec, out_shape=jax.ShapeDtypeStruct((slots, D), BF16),
        compiler_params=pltpu.CompilerParams(
            dimension_semantics=("arbitrary",), vmem_limit_bytes=VMEM_LIMIT),
        name="moe_ffn",
    )(block_e, n_used, xs, wts["w_gu"], wts["b_gu"], wts["w_down"], wts["b_down"])


def _final_ln_kernel(h1_ref, gate_ref, y0_ref, y1_ref, y2_ref, y3_ref, g_ref, b_ref, o_ref):
    gate = gate_ref[...]
    ffn = None
    for kk, y_ref in enumerate((y0_ref, y1_ref, y2_ref, y3_ref)):
        term = gate[:, kk:kk + 1] * y_ref[...].astype(F32)
        ffn = term if ffn is None else ffn + term
    o_ref[...] = _layer_norm(ALPHA * h1_ref[...] + ffn, g_ref[...], b_ref[...])


def _final_ln(h1, gates, ys, g, b, tm):
    T, D = h1.shape
    rows = lambda w: pl.BlockSpec((tm, w), lambda i: (i, 0))
    return pl.pallas_call(
        _final_ln_kernel, grid=(T // tm,),
        in_specs=[rows(D), rows(LANE)] + [rows(D)] * TOP_K +
                 [pl.BlockSpec((1, D), lambda i: (0, 0)), pl.BlockSpec((1, D), lambda i: (0, 0))],
        out_specs=rows(D),
        out_shape=jax.ShapeDtypeStruct((T, D), F32),
        compiler_params=pltpu.CompilerParams(
            dimension_semantics=("parallel",), vmem_limit_bytes=VMEM_LIMIT),
        name="final_ln",
    )(h1, gates, *ys, g, b)


def _prep_weights(p):
    w_in = p["w_in"][0]
    sizes = (GK, GK, GV, GLA_RANK, GLA_RANK, GV, MLA_QR, MLA_KVR, MLA_ROPE)
    offs = [0]
    for s in sizes:
        offs.append(offs[-1] + s)
    wq, wk_, wv, wlf, wlb, wg_, wcq, wckv, wkr = [w_in[:, offs[j]:offs[j + 1]] for j in range(9)]
    D = w_in.shape[0]
    half = MLA_ROPE // 2
    z = lambda n: jnp.zeros((D, n), F32)
    kra = jnp.concatenate([z(MLA_NOPE), wkr, z(HEAD_PAD - MLA_NOPE - MLA_ROPE)], axis=1)
    krb = jnp.concatenate([z(MLA_NOPE), -wkr[:, half:], wkr[:, :half], z(HEAD_PAD - MLA_NOPE - MLA_ROPE)], axis=1)
    w1 = jnp.concatenate([wq * (GLA_DK ** -0.5), wk_, wv, wg_, wcq, wckv, kra, krb, wlf, wlb,
                          z(C_END - C_LR - 2 * GLA_RANK)], axis=1).astype(BF16)
    wg = jnp.zeros((LANE, 2 * GK), F32)
    wg = wg.at[0:GLA_RANK, 0:GK].set(p["gla_gate_w_fwd"][0])
    wg = wg.at[GLA_RANK:2 * GLA_RANK, GK:].set(p["gla_gate_w_bwd"][0]).astype(BF16)
    bg = jnp.concatenate([p["gla_gate_b_fwd"][0], p["gla_gate_b_bwd"][0]])[None, :]
    c = (MLA_NOPE + MLA_ROPE) ** -0.5 * math.log2(math.e)
    wuq = p["mla_w_uq"][0].reshape(MLA_QR, MLA_HEADS, MLA_NOPE + MLA_ROPE) * c
    nope, rope = wuq[:, :, :MLA_NOPE], wuq[:, :, MLA_NOPE:]
    zq = jnp.zeros((MLA_QR, MLA_HEADS, HEAD_PAD - MLA_NOPE - MLA_ROPE), F32)
    wa = jnp.concatenate([nope, rope, zq], axis=2).reshape(MLA_QR, MQ)
    wb = jnp.concatenate([jnp.zeros_like(nope), -rope[:, :, half:], rope[:, :, :half], zq], axis=2).reshape(MLA_QR, MQ)
    wukv = p["mla_w_ukv"][0].reshape(MLA_KVR, MLA_HEADS, MLA_NOPE + MLA_DV)
    wk = jnp.concatenate([wukv[:, :, :MLA_NOPE], jnp.zeros((MLA_KVR, MLA_HEADS, HEAD_PAD - MLA_NOPE), F32)],
                         axis=2).reshape(MLA_KVR, MQ)
    wv_ = wukv[:, :, MLA_NOPE:].reshape(MLA_KVR, MV)
    w_out = p["w_out"][0]
    rw = jnp.zeros((D, LANE), F32).at[:, :N_EXPERTS].set(p["router_w"][0]).astype(BF16)
    rb = jnp.zeros((1, LANE), F32).at[0, :N_EXPERTS].set(p["router_b"][0])
    row = lambda a: a.reshape(1, -1)
    return dict(
        ln_in_g=row(p["ln_in_g"]), ln_in_b=row(p["ln_in_b"]), w1=w1, wg=wg, bg=bg,
        qng=row(p["mla_q_norm_g"][0]), kvng=row(p["mla_kv_norm_g"][0]),
        waT=wa.T.astype(BF16), wbT=wb.T.astype(BF16), wk=wk.astype(BF16), wvT=wv_.T.astype(BF16),
        gla_norm_g=row(p["gla_norm_g"][0]), wo_gla=w_out[:GV].astype(BF16), wo_mla=w_out[GV:].astype(BF16),
        ln1_g=row(p["ln1_g"][0]), ln1_b=row(p["ln1_b"][0]), router_w=rw, router_b=rb,
        w_gu=p["expert_w_gu"][0], b_gu=p["expert_b_gu"][0][:, None, :],
        w_down=p["expert_w_down"][0], b_down=p["expert_b_down"][0][:, None, :],
        ln2_g=row(p["ln2_g"][0]), ln2_b=row(p["ln2_b"][0]),
    )


def _rope_tables(start, n):
    half = MLA_ROPE // 2
    inv = ROPE_BASE ** (-jnp.arange(0, MLA_ROPE, 2, dtype=F32) / MLA_ROPE)
    ang = (jnp.arange(n, dtype=F32) + float(start))[:, None] * inv[None, :]
    cos, sin = jnp.cos(ang), jnp.sin(ang)
    cos2 = jnp.concatenate([cos, cos], axis=1)
    sin2 = jnp.concatenate([sin, sin], axis=1)
    tail = jnp.zeros((n, HEAD_PAD - MLA_NOPE - MLA_ROPE), F32)
    cosq = jnp.concatenate([jnp.ones((n, MLA_NOPE), F32), cos2, tail], axis=1)
    rot = jnp.concatenate([jnp.zeros((n, MLA_NOPE), F32), sin2, tail], axis=1)
    cosk = jnp.concatenate([jnp.zeros((n, MLA_NOPE), F32), cos2, tail], axis=1)
    return dict(cosqT=cosq.T, sinqT=rot.T, cosk=cosk, sink=rot)


def _route(top_idx, rank, counts, bm):
    T = top_idx.shape[0]
    n_assign = T * TOP_K
    pcounts = (counts + bm - 1) // bm * bm
    pends = jnp.cumsum(pcounts)
    pstarts = pends - pcounts
    dest = jnp.take(pstarts, top_idx, axis=0) + rank
    n_blocks = -(-n_assign // bm) + N_EXPERTS
    slots = n_blocks * bm
    slot_tok = jnp.zeros((slots,), jnp.int32).at[dest.reshape(-1)].set(
        jnp.arange(n_assign, dtype=jnp.int32) // TOP_K)
    block_e = jnp.minimum(
        jnp.searchsorted(pends, jnp.arange(n_blocks, dtype=pends.dtype) * bm, side="right"),
        N_EXPERTS - 1).astype(jnp.int32)
    n_used = (pends[-1:] // bm).astype(jnp.int32)
    return slot_tok, block_e, n_used, dest


def _trunk(x, wts, meta):
    B, S, D = x.shape
    tm = min(512, S)
    qkv, lg, g, qT, k, vT = _in_proj(x, wts, _rope_tables(N_META, S), tm)
    o_f, o_b = _gla(qkv, lg, meta["qkv"], meta["lg"], tm)
    mla = _mla(qT, k, vT, meta["k"], meta["vT"])
    h1, h1b, route, gate, cnt = _out_proj(x, o_f, o_b, g, mla, wts, tm)
    T = B * S
    h1 = h1.reshape(T, D)
    h1b = h1b.reshape(T, D)
    route = route.reshape(T, LANE)
    slot_tok, block_e, n_used, dest = _route(
        route[:, :TOP_K], route[:, TOP_K:2 * TOP_K], cnt[0, :N_EXPERTS], MOE_BLOCK)
    xs = jnp.take(h1b, slot_tok, axis=0)
    y = _moe_ffn(xs, block_e, n_used, wts)
    ys = [jnp.take(y, dest[:, kk], axis=0) for kk in range(TOP_K)]
    out = _final_ln(h1, gate.reshape(T, LANE), ys, wts["ln2_g"], wts["ln2_b"], tm)
    return out.reshape(B, S, D)


def _meta_prep(meta_tokens, wts):
    xm = jnp.zeros((1, LANE, D_MODEL), F32).at[0, :N_META].set(meta_tokens)
    qkv, lg, _, _, k, vT = _in_proj(xm, wts, _rope_tables(0, LANE), LANE)
    front = GLA_CHUNK - N_META
    mq = jnp.zeros((GLA_CHUNK, C_G), F32).at[front:].set(qkv[0, :N_META])
    ml = jnp.zeros((GLA_CHUNK, GK), F32).at[front:].set(lg[0, :N_META, :GK])
    return dict(qkv=mq, lg=ml, k=k[0], vT=vT[0, 0])


def kernel(x_prompt, x_sample, meta_tokens, ln_in_g, ln_in_b, w_in, gla_gate_w_fwd, gla_gate_b_fwd,
           gla_gate_w_bwd, gla_gate_b_bwd, gla_norm_g, mla_q_norm_g, mla_w_uq, mla_kv_norm_g, mla_w_ukv,
           w_out, ln1_g, ln1_b, router_w, router_b, expert_w_gu, expert_b_gu, expert_w_down, expert_b_down,
           ln2_g, ln2_b):
    wts = _prep_weights(dict(
        ln_in_g=ln_in_g, ln_in_b=ln_in_b, w_in=w_in, gla_gate_w_fwd=gla_gate_w_fwd,
        gla_gate_b_fwd=gla_gate_b_fwd, gla_gate_w_bwd=gla_gate_w_bwd, gla_gate_b_bwd=gla_gate_b_bwd,
        gla_norm_g=gla_norm_g, mla_q_norm_g=mla_q_norm_g, mla_w_uq=mla_w_uq, mla_kv_norm_g=mla_kv_norm_g,
        mla_w_ukv=mla_w_ukv, w_out=w_out, ln1_g=ln1_g, ln1_b=ln1_b, router_w=router_w, router_b=router_b,
        expert_w_gu=expert_w_gu, expert_b_gu=expert_b_gu, expert_w_down=expert_w_down,
        expert_b_down=expert_b_down, ln2_g=ln2_g, ln2_b=ln2_b))
    meta = _meta_prep(meta_tokens, wts)
    return (_trunk(x_prompt, wts, meta), _trunk(x_sample, wts, meta))
```

```python
import functools
import math

import jax
import jax.numpy as jnp
from jax import lax
from jax.experimental import pallas as pl
from jax.experimental.pallas import tpu as pltpu

D_MODEL = 1024
N_META = 16
GLA_HEADS, GLA_DK, GLA_DV, GLA_RANK = 4, 64, 128, 16
GLA_TAU = 16.0
GLA_CHUNK = 64
MLA_HEADS, MLA_QR, MLA_KVR, MLA_NOPE, MLA_ROPE, MLA_DV = 8, 256, 128, 64, 32, 64
ROPE_BASE = 10000.0
N_EXPERTS, TOP_K, D_FF = 32, 4, 1024
SWIGLU_LIMIT, SWIGLU_ALPHA = 7.0, 1.702
DEPTH = 1
ALPHA = (2.0 * DEPTH) ** 0.25
LN_EPS, RMS_EPS = 1e-5, 1e-6

LANE = 128
HEAD_PAD = 128
GK = GLA_HEADS * GLA_DK
GV = GLA_HEADS * GLA_DV
MV = MLA_HEADS * MLA_DV
VT_ROWS = MLA_DV + 16
MVT = MLA_HEADS * VT_ROWS
MQ = MLA_HEADS * HEAD_PAD
C_Q, C_K, C_V, C_G, C_CQ, C_CKV, C_KRA, C_KRB, C_LR, C_END = (
    0, 256, 512, 1024, 1536, 1792, 1920, 2048, 2176, 2304)
MOE_BLOCK = 256
NEG_BIG = -1e30
VMEM_LIMIT = 56 * 1024 * 1024

BF16 = jnp.bfloat16
F32 = jnp.float32


def _dot(a, b):
    return jnp.dot(a, b, preferred_element_type=F32)


def _dot_nt(a, b):
    return lax.dot_general(a, b, (((1,), (1,)), ((), ())), preferred_element_type=F32)


def _dot_tn(a, b):
    return lax.dot_general(a, b, (((0,), (0,)), ((), ())), preferred_element_type=F32)


def _layer_norm(x, g, b):
    mu = jnp.mean(x, axis=-1, keepdims=True)
    xc = x - mu
    var = jnp.mean(xc * xc, axis=-1, keepdims=True)
    return xc * lax.rsqrt(var + LN_EPS) * g + b


def _rms_norm(x, g):
    ms = jnp.mean(x * x, axis=-1, keepdims=True)
    return x * lax.rsqrt(ms + RMS_EPS) * g


def _log_sigmoid(z):
    return jnp.minimum(z, 0.0) - jnp.log(1.0 + jnp.exp(-jnp.abs(z)))


def _in_proj_kernel(x_ref, lng_ref, lnb_ref, w1_ref, wg_ref, bg_ref, qng_ref, kvng_ref,
                    waT_ref, wbT_ref, wk_ref, wvT_ref, cosqT_ref, sinqT_ref, cosk_ref, sink_ref,
                    qkv_ref, lg_ref, g_ref, qT_ref, k_ref, vT_ref):
    xn = _layer_norm(x_ref[...], lng_ref[...], lnb_ref[...])
    p = _dot(xn.astype(BF16), w1_ref[...])
    qkv_ref[...] = p[:, C_Q:C_G]
    g_ref[...] = p[:, C_G:C_CQ].astype(BF16)
    z = _dot(p[:, C_LR:C_END].astype(BF16), wg_ref[...]) + bg_ref[...]
    lg_ref[...] = _log_sigmoid(z) * (1.0 / GLA_TAU)
    cqn = _rms_norm(p[:, C_CQ:C_CKV], qng_ref[...]).astype(BF16)
    qaT = _dot_nt(waT_ref[...], cqn)
    qbT = _dot_nt(wbT_ref[...], cqn)
    cosq, sinq = cosqT_ref[...], sinqT_ref[...]
    for h in range(MLA_HEADS):
        sl = slice(h * HEAD_PAD, (h + 1) * HEAD_PAD)
        qT_ref[sl, :] = (qaT[sl, :] * cosq + qbT[sl, :] * sinq).astype(BF16)
    ckvn = _rms_norm(p[:, C_CKV:C_KRA], kvng_ref[...]).astype(BF16)
    knp = _dot(ckvn, wk_ref[...])
    krope = p[:, C_KRA:C_KRB] * cosk_ref[...] + p[:, C_KRB:C_LR] * sink_ref[...]
    for h in range(MLA_HEADS):
        sl = slice(h * HEAD_PAD, (h + 1) * HEAD_PAD)
        k_ref[:, sl] = (knp[:, sl] + krope).astype(BF16)
    vT = _dot_nt(wvT_ref[...], ckvn).astype(BF16)
    tail_rows = VT_ROWS - MLA_DV
    ones_row = jnp.where(lax.broadcasted_iota(jnp.int32, (tail_rows, vT.shape[1]), 0) == 0, 1.0, 0.0).astype(BF16)
    for h in range(MLA_HEADS):
        vT_ref[h * VT_ROWS:h * VT_ROWS + MLA_DV, :] = vT[h * MLA_DV:(h + 1) * MLA_DV, :]
        vT_ref[h * VT_ROWS + MLA_DV:(h + 1) * VT_ROWS, :] = ones_row


def _in_proj(x, wts, tabs, tm):
    B, S, D = x.shape
    n = S // tm
    const = lambda shape: pl.BlockSpec(shape, lambda b, i: (0,) * len(shape))
    in_specs = [
        pl.BlockSpec((None, tm, D), lambda b, i: (b, i, 0)),
        const((1, D)), const((1, D)),
        const((D, C_END)), const((LANE, 2 * GK)), const((1, 2 * GK)),
        const((1, MLA_QR)), const((1, MLA_KVR)),
        const((MQ, MLA_QR)), const((MQ, MLA_QR)),
        const((MLA_KVR, MQ)), const((MV, MLA_KVR)),
        pl.BlockSpec((HEAD_PAD, tm), lambda b, i: (0, i)),
        pl.BlockSpec((HEAD_PAD, tm), lambda b, i: (0, i)),
        pl.BlockSpec((tm, HEAD_PAD), lambda b, i: (i, 0)),
        pl.BlockSpec((tm, HEAD_PAD), lambda b, i: (i, 0)),
    ]
    out_shape = [
        jax.ShapeDtypeStruct((B, S, C_G), F32),
        jax.ShapeDtypeStruct((B, S, 2 * GK), F32),
        jax.ShapeDtypeStruct((B, S, GV), BF16),
        jax.ShapeDtypeStruct((B, n, MQ, tm), BF16),
        jax.ShapeDtypeStruct((B, S, MQ), BF16),
        jax.ShapeDtypeStruct((B, n, MVT, tm), BF16),
    ]
    out_specs = [
        pl.BlockSpec((None, tm, C_G), lambda b, i: (b, i, 0)),
        pl.BlockSpec((None, tm, 2 * GK), lambda b, i: (b, i, 0)),
        pl.BlockSpec((None, tm, GV), lambda b, i: (b, i, 0)),
        pl.BlockSpec((None, None, MQ, tm), lambda b, i: (b, i, 0, 0)),
        pl.BlockSpec((None, tm, MQ), lambda b, i: (b, i, 0)),
        pl.BlockSpec((None, None, MVT, tm), lambda b, i: (b, i, 0, 0)),
    ]
    return pl.pallas_call(
        _in_proj_kernel, grid=(B, n), in_specs=in_specs, out_specs=out_specs, out_shape=out_shape,
        compiler_params=pltpu.CompilerParams(
            dimension_semantics=("parallel", "parallel"), vmem_limit_bytes=VMEM_LIMIT),
        name="in_proj",
    )(x, wts["ln_in_g"], wts["ln_in_b"], wts["w1"], wts["wg"], wts["bg"], wts["qng"], wts["kvng"],
      wts["waT"], wts["wbT"], wts["wk"], wts["wvT"], tabs["cosqT"], tabs["sinqT"], tabs["cosk"], tabs["sink"])


GLA_LEVELS = tuple(GLA_CHUNK >> (j + 1) for j in range(GLA_CHUNK.bit_length() - 1))
GLA_SEG_BLOCKS = 2 + len(GLA_LEVELS)


def _gla_tables(reverse):
    import numpy as np
    C = GLA_CHUNK
    t = np.arange(C)[:, None]
    r = np.arange(C)[None, :]
    blocks = [(r <= t), (r > t)]
    masks = []
    for m in GLA_LEVELS:
        mid = (t // (2 * m)) * 2 * m + m - 1
        later = (t % (2 * m)) >= m
        blocks.append(np.where(later, (r > mid) & (r <= t), (r > t) & (r <= mid)))
        same_pair = (t // (2 * m)) == (r // (2 * m))
        masks.append(same_pair & later & ((r % (2 * m)) < m))
    masks.append(t == r)
    seg = np.concatenate([b.astype(np.float32) for b in blocks], axis=0)
    msk = np.stack([np.tile(mk.astype(np.float32), (1, GLA_HEADS)) for mk in masks])
    if reverse:
        seg = seg.reshape(GLA_SEG_BLOCKS, C, C)[:, ::-1, ::-1].reshape(GLA_SEG_BLOCKS * C, C)
        msk = msk.reshape(len(masks), C, GLA_HEADS, C)[:, ::-1, :, ::-1].reshape(len(masks), C, GLA_HEADS * C)
    return jnp.asarray(seg, BF16), jnp.asarray(msk, F32)


def _gla_chunk(q, k, v, lg, s_t, seg, msk_ref, reverse, need_out=True):
    C = q.shape[0]
    hi = lg.astype(BF16)
    lo = (lg - hi.astype(F32)).astype(BF16)
    e = jnp.exp(_dot(seg, hi) + _dot(seg, lo))
    blk = lambda j: e[j * C:(j + 1) * C, :]
    total = blk(0)[0:1, :] if reverse else blk(0)[C - 1:C, :]
    o = None
    if need_out:
        khead = lax.broadcasted_iota(jnp.int32, (C, GK), 1) // GLA_DK
        vhead = lax.broadcasted_iota(jnp.int32, (C, GV), 1) // GLA_DV
        qb, kb = q.astype(BF16), k.astype(BF16)
        zk = jnp.zeros((C, GK), BF16)

        def scores(qs, ks):
            kexp = jnp.concatenate([jnp.where(khead == h, ks, zk) for h in range(GLA_HEADS)], axis=0)
            return _dot_nt(qs, kexp)

        a = None
        for j in range(len(GLA_LEVELS)):
            ej = blk(2 + j)
            term = scores((q * ej).astype(BF16), (k * ej).astype(BF16)) * msk_ref[j]
            a = term if a is None else a + term
        if not reverse:
            a = a + scores(qb, kb) * msk_ref[len(GLA_LEVELS)]
        vb = v.astype(BF16)
        zv = jnp.zeros((C, GV), BF16)
        vexp = jnp.concatenate([jnp.where(vhead == h, vb, zv) for h in range(GLA_HEADS)], axis=0)
        o = _dot(a.astype(BF16), vexp) + _dot_nt((q * blk(0)).astype(BF16), s_t.astype(BF16))
    upd = _dot_tn(v.astype(BF16), (k * blk(1)).astype(BF16))
    srow = lax.broadcasted_iota(jnp.int32, (GV, GK), 0) // GLA_DV
    scol = lax.broadcasted_iota(jnp.int32, (GV, GK), 1) // GLA_DK
    s_new = s_t * total + jnp.where(srow == scol, upd, 0.0)
    return o, s_new


def _gla_kernel(qkv_f_ref, lg_f_ref, qkv_b_ref, lg_b_ref, mqkv_ref, mlg_ref,
                segf_ref, mskf_ref, segb_ref, mskb_ref,
                of_ref, ob_ref, sf_ref, sb_ref, *, nch):
    C = GLA_CHUNK

    @pl.when(pl.program_id(1) == 0)
    def _():
        m = mqkv_ref[...]
        _, s0 = _gla_chunk(m[:, C_Q:C_K], m[:, C_K:C_V], m[:, C_V:C_G], mlg_ref[...],
                           jnp.zeros((GV, GK), F32), segf_ref[...], mskf_ref, reverse=False, need_out=False)
        sf_ref[...] = s0
        sb_ref[...] = jnp.zeros((GV, GK), F32)

    def body(c, carry):
        rf = pl.multiple_of(c * C, C)
        rb = pl.multiple_of((nch - 1 - c) * C, C)
        o, s = _gla_chunk(qkv_f_ref[pl.ds(rf, C), C_Q:C_K], qkv_f_ref[pl.ds(rf, C), C_K:C_V],
                          qkv_f_ref[pl.ds(rf, C), C_V:C_G], lg_f_ref[pl.ds(rf, C), :],
                          sf_ref[...], segf_ref[...], mskf_ref, reverse=False)
        of_ref[pl.ds(rf, C), :] = o
        sf_ref[...] = s
        o, s = _gla_chunk(qkv_b_ref[pl.ds(rb, C), C_Q:C_K], qkv_b_ref[pl.ds(rb, C), C_K:C_V],
                          qkv_b_ref[pl.ds(rb, C), C_V:C_G], lg_b_ref[pl.ds(rb, C), :],
                          sb_ref[...], segb_ref[...], mskb_ref, reverse=True)
        ob_ref[pl.ds(rb, C), :] = o
        sb_ref[...] = s
        return carry

    lax.fori_loop(0, nch, body, 0)


def _gla(qkv, lg, meta_qkv, meta_lg, rows):
    B, S, _ = qkv.shape
    n = S // rows
    seg_f, msk_f = _gla_tables(False)
    seg_b, msk_b = _gla_tables(True)
    const = lambda a: pl.BlockSpec(a.shape, lambda b, i: (0,) * a.ndim)
    in_specs = [
        pl.BlockSpec((None, rows, C_G), lambda b, i: (b, i, 0)),
        pl.BlockSpec((None, rows, GK), lambda b, i: (b, i, 0)),
        pl.BlockSpec((None, rows, C_G), lambda b, i: (b, n - 1 - i, 0)),
        pl.BlockSpec((None, rows, GK), lambda b, i: (b, n - 1 - i, 1)),
        pl.BlockSpec((GLA_CHUNK, C_G), lambda b, i: (0, 0)),
        pl.BlockSpec((GLA_CHUNK, GK), lambda b, i: (0, 0)),
        const(seg_f), const(msk_f), const(seg_b), const(msk_b),
    ]
    out_specs = [
        pl.BlockSpec((None, rows, GV), lambda b, i: (b, i, 0)),
        pl.BlockSpec((None, rows, GV), lambda b, i: (b, n - 1 - i, 0)),
    ]
    out_shape = [jax.ShapeDtypeStruct((B, S, GV), F32)] * 2
    return pl.pallas_call(
        functools.partial(_gla_kernel, nch=rows // GLA_CHUNK),
        grid=(B, n), in_specs=in_specs, out_specs=out_specs, out_shape=out_shape,
        scratch_shapes=[pltpu.VMEM((GV, GK), F32), pltpu.VMEM((GV, GK), F32)],
        compiler_params=pltpu.CompilerParams(
            dimension_semantics=("parallel", "arbitrary"), vmem_limit_bytes=VMEM_LIMIT),
        name="gla",
    )(qkv, lg, qkv, lg, meta_qkv, meta_lg, seg_f, msk_f, seg_b, msk_b)


def _mla_kernel(qT_ref, k_ref, vT_ref, km_ref, vTm_ref, o_ref, s_ref, m_ref, acc_ref, *, nk, tk):
    ksl = [slice(hh * HEAD_PAD, (hh + 1) * HEAD_PAD) for hh in range(2)]
    vsl = [slice(hh * VT_ROWS, (hh + 1) * VT_ROWS) for hh in range(2)]

    for hh in range(2):
        s = _dot(km_ref[:, ksl[hh]], qT_ref[ksl[hh], :])
        valid = lax.broadcasted_iota(jnp.int32, s.shape, 0) < N_META
        s = jnp.where(valid, s, NEG_BIG)
        m = jnp.max(s, axis=0, keepdims=True)
        m_ref[hh] = m
        acc_ref[hh] = _dot(vTm_ref[vsl[hh], :], jnp.exp2(s - m).astype(BF16))

    def scores(j, slot):
        r0 = pl.multiple_of(j * tk, tk)
        for hh in range(2):
            s_ref[slot, hh] = _dot(k_ref[pl.ds(r0, tk), ksl[hh]], qT_ref[ksl[hh], :])

    def consume(j, slot):
        for hh in range(2):
            s = s_ref[slot, hh]
            m = m_ref[hh]
            m_new = jnp.maximum(m, jnp.max(s, axis=0, keepdims=True))
            p = jnp.exp2(s - m_new).astype(BF16)
            acc_ref[hh] = jnp.exp2(m - m_new) * acc_ref[hh] + _dot(vT_ref[j, vsl[hh], :], p)
            m_ref[hh] = m_new

    scores(0, 0)

    def body(i, carry):
        scores(2 * i + 1, 1)
        consume(2 * i, 0)
        scores(jnp.minimum(2 * i + 2, nk - 1), 0)
        consume(2 * i + 1, 1)
        return carry

    lax.fori_loop(0, nk // 2, body, 0)
    res = []
    for hh in range(2):
        acc = acc_ref[hh]
        res.append(acc[:MLA_DV, :] * (1.0 / acc[MLA_DV:MLA_DV + 1, :]))
    out = jnp.concatenate(res, axis=0)
    o_ref[...] = out.T.astype(o_ref.dtype)


def _mla(qT, k, vT, k_meta, vT_meta):
    B, n, _, t = qT.shape
    S = k.shape[1]
    assert n % 2 == 0
    npair = MLA_HEADS // 2
    in_specs = [
        pl.BlockSpec((None, None, 2 * HEAD_PAD, t), lambda b, hp, i: (b, i, hp, 0)),
        pl.BlockSpec((None, S, 2 * HEAD_PAD), lambda b, hp, i: (b, 0, hp)),
        pl.BlockSpec((None, n, 2 * VT_ROWS, t), lambda b, hp, i: (b, 0, hp, 0)),
        pl.BlockSpec((LANE, 2 * HEAD_PAD), lambda b, hp, i: (0, hp)),
        pl.BlockSpec((2 * VT_ROWS, LANE), lambda b, hp, i: (hp, 0)),
    ]
    out_specs = pl.BlockSpec((None, t, 2 * MLA_DV), lambda b, hp, i: (b, i, hp))
    return pl.pallas_call(
        functools.partial(_mla_kernel, nk=n, tk=t),
        grid=(B, npair, n), in_specs=in_specs, out_specs=out_specs,
        out_shape=jax.ShapeDtypeStruct((B, S, MV), BF16),
        scratch_shapes=[pltpu.VMEM((2, 2, t, t), F32), pltpu.VMEM((2, 1, t), F32),
                        pltpu.VMEM((2, VT_ROWS, t), F32)],
        compiler_params=pltpu.CompilerParams(
            dimension_semantics=("parallel", "parallel", "parallel"), vmem_limit_bytes=VMEM_LIMIT),
        name="mla",
    )(qT, k, vT, k_meta, vT_meta)


def _out_proj_kernel(x_ref, of_ref, ob_ref, g_ref, mla_ref, lng_ref, lnb_ref, gng_ref,
                     wog_ref, wom_ref, l1g_ref, l1b_ref, rw_ref, rb_ref, cnt0_ref,
                     h1_ref, h1b_ref, route_ref, gate_ref, cnt_ref, run_ref):
    @pl.when(jnp.logical_and(pl.program_id(0) == 0, pl.program_id(1) == 0))
    def _():
        run_ref[...] = cnt0_ref[...].astype(F32)

    h0 = _layer_norm(x_ref[...], lng_ref[...], lnb_ref[...])
    o = of_ref[...] + ob_ref[...]
    gng = gng_ref[...]
    parts = []
    for h in range(GLA_HEADS):
        sl = slice(h * GLA_DV, (h + 1) * GLA_DV)
        parts.append(_rms_norm(o[:, sl], gng[:, sl]))
    g = g_ref[...].astype(F32)
    gla = jnp.concatenate(parts, axis=1) * (g * (1.0 / (1.0 + jnp.exp(-g))))
    mix = _dot(gla.astype(BF16), wog_ref[...]) + _dot(mla_ref[...], wom_ref[...])
    h1 = _layer_norm(ALPHA * h0 + mix, l1g_ref[...], l1b_ref[...])
    h1_ref[...] = h1
    h1b = h1.astype(BF16)
    h1b_ref[...] = h1b
    logits = _dot(h1b, rw_ref[...]) + rb_ref[...]
    lane = lax.broadcasted_iota(jnp.int32, logits.shape, 1)
    vals = jnp.where(lane < N_EXPERTS, logits, -jnp.inf)
    route = jnp.zeros(logits.shape, jnp.int32)
    onehot = jnp.zeros(logits.shape, F32)
    top, sel = [], []
    for kk in range(TOP_K):
        mx = jnp.max(vals, axis=-1, keepdims=True)
        idx = jnp.min(jnp.where(vals == mx, lane, LANE), axis=-1, keepdims=True)
        top.append(mx)
        hit = lane == idx
        sel.append(hit)
        route = jnp.where(lane == kk, idx, route)
        onehot = jnp.where(hit, 1.0, onehot)
        vals = jnp.where(hit, -jnp.inf, vals)
    tm = logits.shape[0]
    earlier = (lax.broadcasted_iota(jnp.int32, (tm, tm), 1) < lax.broadcasted_iota(jnp.int32, (tm, tm), 0))
    before = _dot(jnp.where(earlier, 1.0, 0.0).astype(BF16), onehot.astype(BF16)) + run_ref[...]
    for kk in range(TOP_K):
        rk = jnp.sum(jnp.where(sel[kk], before, 0.0), axis=-1, keepdims=True)
        route = jnp.where(lane == TOP_K + kk, rk.astype(jnp.int32), route)
    run_ref[...] = run_ref[...] + jnp.sum(onehot, axis=0, keepdims=True)
    cnt_ref[...] = run_ref[...].astype(jnp.int32)
    es = [jnp.exp(t - top[0]) for t in top]
    inv = 1.0 / (es[0] + es[1] + es[2] + es[3])
    gate_out = jnp.zeros(logits.shape, F32)
    for kk in range(TOP_K):
        gate_out = jnp.where(lane == kk, es[kk] * inv, gate_out)
    route_ref[...] = route
    gate_ref[...] = gate_out


def _out_proj(x, o_f, o_b, g, mla, cnt0, wts, tm):
    B, S, D = x.shape
    n = S // tm
    const = lambda shape: pl.BlockSpec(shape, lambda b, i: (0,) * len(shape))
    row = lambda w: pl.BlockSpec((None, tm, w), lambda b, i: (b, i, 0))
    in_specs = [row(D), row(GV), row(GV), row(GV), row(MV),
                const((1, D)), const((1, D)), const((1, GV)),
                const((GV, D)), const((MV, D)), const((1, D)), const((1, D)),
                const((D, LANE)), const((1, LANE)), const((1, LANE))]
    out_shape = [jax.ShapeDtypeStruct((B, S, D), F32), jax.ShapeDtypeStruct((B, S, D), BF16),
                 jax.ShapeDtypeStruct((B, S, LANE), jnp.int32), jax.ShapeDtypeStruct((B, S, LANE), F32),
                 jax.ShapeDtypeStruct((1, LANE), jnp.int32)]
    out_specs = [row(D), row(D), row(LANE), row(LANE), const((1, LANE))]
    return pl.pallas_call(
        _out_proj_kernel, grid=(B, n), in_specs=in_specs, out_specs=out_specs, out_shape=out_shape,
        scratch_shapes=[pltpu.VMEM((1, LANE), F32)],
        compiler_params=pltpu.CompilerParams(
            dimension_semantics=("arbitrary", "arbitrary"), vmem_limit_bytes=VMEM_LIMIT),
        name="out_proj",
    )(x, o_f, o_b, g, mla, wts["ln_in_g"], wts["ln_in_b"], wts["gla_norm_g"], wts["wo_gla"], wts["wo_mla"],
      wts["ln1_g"], wts["ln1_b"], wts["router_w"], wts["router_b"], cnt0)


def _moe_kernel(be_ref, nused_ref, xs_ref, wgu_ref, bgu_ref, wd_ref, bd_ref, y_ref,
                wgu_bf, wd_bf):
    i = pl.program_id(0)
    e = be_ref[i]
    e_prev = be_ref[jnp.maximum(i - 1, 0)]
    used = i < nused_ref[0]

    @pl.when(jnp.logical_and(used, jnp.logical_or(i == 0, e != e_prev)))
    def _():
        wgu_bf[...] = wgu_ref[...].astype(BF16)
        wd_bf[...] = wd_ref[...].astype(BF16)

    @pl.when(used)
    def _():
        gu = _dot(xs_ref[...], wgu_bf[...]) + bgu_ref[...]
        gate = jnp.minimum(gu[:, :D_FF], SWIGLU_LIMIT)
        up = jnp.clip(gu[:, D_FF:], -SWIGLU_LIMIT, SWIGLU_LIMIT)
        act = (up + 1.0) * gate * (1.0 / (1.0 + jnp.exp(-SWIGLU_ALPHA * gate)))
        y = _dot(act.astype(BF16), wd_bf[...]) + bd_ref[...]
        y_ref[...] = y.astype(y_ref.dtype)

    @pl.when(jnp.logical_not(used))
    def _():
        y_ref[...] = jnp.zeros(y_ref.shape, y_ref.dtype)


def _moe_ffn(xs, block_e, n_used, wts):
    slots, D = xs.shape
    bm = MOE_BLOCK
    n_blocks = slots // bm
    grid_spec = pltpu.PrefetchScalarGridSpec(
        num_scalar_prefetch=2, grid=(n_blocks,),
        in_specs=[
            pl.BlockSpec((bm, D), lambda i, be, nu: (i, 0)),
            pl.BlockSpec((None, D, 2 * D_FF), lambda i, be, nu: (be[i], 0, 0)),
            pl.BlockSpec((None, 1, 2 * D_FF), lambda i, be, nu: (be[i], 0, 0)),
            pl.BlockSpec((None, D_FF, D), lambda i, be, nu: (be[i], 0, 0)),
            pl.BlockSpec((None, 1, D), lambda i, be, nu: (be[i], 0, 0)),
        ],
        out_specs=pl.BlockSpec((bm, D), lambda i, be, nu: (i, 0)),
        scratch_shapes=[pltpu.VMEM((D, 2 * D_FF), BF16), pltpu.VMEM((D_FF, D), BF16)],
    )
    return pl.pallas_call(
        _moe_kernel, grid_spec=grid_spec, out_shape=jax.ShapeDtypeStruct((slots, D), BF16),
        compiler_params=pltpu.CompilerParams(
            dimension_semantics=("arbitrary",), vmem_limit_bytes=VMEM_LIMIT),
        name="moe_ffn",
    )(block_e, n_used, xs, wts["w_gu"], wts["b_gu"], wts["w_down"], wts["b_down"])


def _final_ln_kernel(h1_ref, gate_ref, y0_ref, y1_ref, y2_ref, y3_ref, g_ref, b_ref, o_ref):
    gate = gate_ref[...]
    ffn = None
    for kk, y_ref in enumerate((y0_ref, y1_ref, y2_ref, y3_ref)):
        term = gate[:, kk:kk + 1] * y_ref[...].astype(F32)
        ffn = term if ffn is None else ffn + term
    o_ref[...] = _layer_norm(ALPHA * h1_ref[...] + ffn, g_ref[...], b_ref[...])


def _final_ln(h1, gates, ys, row0, g, b, tm):
    T, D = h1.shape
    assert row0 % tm == 0
    rows = lambda w: pl.BlockSpec((tm, w), lambda i: (i, 0))
    yrows = pl.BlockSpec((tm, D), lambda i: (i + row0 // tm, 0))
    return pl.pallas_call(
        _final_ln_kernel, grid=(T // tm,),
        in_specs=[rows(D), rows(LANE)] + [yrows] * TOP_K +
                 [pl.BlockSpec((1, D), lambda i: (0, 0)), pl.BlockSpec((1, D), lambda i: (0, 0))],
        out_specs=rows(D),
        out_shape=jax.ShapeDtypeStruct((T, D), F32),
        compiler_params=pltpu.CompilerParams(
            dimension_semantics=("parallel",), vmem_limit_bytes=VMEM_LIMIT),
        name="final_ln",
    )(h1, gates, *ys, g, b)


def _prep_weights(p):
    w_in = p["w_in"][0]
    sizes = (GK, GK, GV, GLA_RANK, GLA_RANK, GV, MLA_QR, MLA_KVR, MLA_ROPE)
    offs = [0]
    for s in sizes:
        offs.append(offs[-1] + s)
    wq, wk_, wv, wlf, wlb, wg_, wcq, wckv, wkr = [w_in[:, offs[j]:offs[j + 1]] for j in range(9)]
    D = w_in.shape[0]
    half = MLA_ROPE // 2
    z = lambda n: jnp.zeros((D, n), F32)
    kra = jnp.concatenate([z(MLA_NOPE), wkr, z(HEAD_PAD - MLA_NOPE - MLA_ROPE)], axis=1)
    krb = jnp.concatenate([z(MLA_NOPE), -wkr[:, half:], wkr[:, :half], z(HEAD_PAD - MLA_NOPE - MLA_ROPE)], axis=1)
    w1 = jnp.concatenate([wq * (GLA_DK ** -0.5), wk_, wv, wg_, wcq, wckv, kra, krb, wlf, wlb,
                          z(C_END - C_LR - 2 * GLA_RANK)], axis=1).astype(BF16)
    wg = jnp.zeros((LANE, 2 * GK), F32)
    wg = wg.at[0:GLA_RANK, 0:GK].set(p["gla_gate_w_fwd"][0])
    wg = wg.at[GLA_RANK:2 * GLA_RANK, GK:].set(p["gla_gate_w_bwd"][0]).astype(BF16)
    bg = jnp.concatenate([p["gla_gate_b_fwd"][0], p["gla_gate_b_bwd"][0]])[None, :]
    c = (MLA_NOPE + MLA_ROPE) ** -0.5 * math.log2(math.e)
    wuq = p["mla_w_uq"][0].reshape(MLA_QR, MLA_HEADS, MLA_NOPE + MLA_ROPE) * c
    nope, rope = wuq[:, :, :MLA_NOPE], wuq[:, :, MLA_NOPE:]
    zq = jnp.zeros((MLA_QR, MLA_HEADS, HEAD_PAD - MLA_NOPE - MLA_ROPE), F32)
    wa = jnp.concatenate([nope, rope, zq], axis=2).reshape(MLA_QR, MQ)
    wb = jnp.concatenate([jnp.zeros_like(nope), -rope[:, :, half:], rope[:, :, :half], zq], axis=2).reshape(MLA_QR, MQ)
    wukv = p["mla_w_ukv"][0].reshape(MLA_KVR, MLA_HEADS, MLA_NOPE + MLA_DV)
    wk = jnp.concatenate([wukv[:, :, :MLA_NOPE], jnp.zeros((MLA_KVR, MLA_HEADS, HEAD_PAD - MLA_NOPE), F32)],
                         axis=2).reshape(MLA_KVR, MQ)
    wv_ = wukv[:, :, MLA_NOPE:].reshape(MLA_KVR, MV)
    w_out = p["w_out"][0]
    rw = jnp.zeros((D, LANE), F32).at[:, :N_EXPERTS].set(p["router_w"][0]).astype(BF16)
    rb = jnp.zeros((1, LANE), F32).at[0, :N_EXPERTS].set(p["router_b"][0])
    row = lambda a: a.reshape(1, -1)
    return dict(
        ln_in_g=row(p["ln_in_g"]), ln_in_b=row(p["ln_in_b"]), w1=w1, wg=wg, bg=bg,
        qng=row(p["mla_q_norm_g"][0]), kvng=row(p["mla_kv_norm_g"][0]),
        waT=wa.T.astype(BF16), wbT=wb.T.astype(BF16), wk=wk.astype(BF16), wvT=wv_.T.astype(BF16),
        gla_norm_g=row(p["gla_norm_g"][0]), wo_gla=w_out[:GV].astype(BF16), wo_mla=w_out[GV:].astype(BF16),
        ln1_g=row(p["ln1_g"][0]), ln1_b=row(p["ln1_b"][0]), router_w=rw, router_b=rb,
        w_gu=p["expert_w_gu"][0], b_gu=p["expert_b_gu"][0][:, None, :],
        w_down=p["expert_w_down"][0], b_down=p["expert_b_down"][0][:, None, :],
        ln2_g=row(p["ln2_g"][0]), ln2_b=row(p["ln2_b"][0]),
    )


def _rope_tables(start, n):
    half = MLA_ROPE // 2
    inv = ROPE_BASE ** (-jnp.arange(0, MLA_ROPE, 2, dtype=F32) / MLA_ROPE)
    ang = (jnp.arange(n, dtype=F32) + float(start))[:, None] * inv[None, :]
    cos, sin = jnp.cos(ang), jnp.sin(ang)
    cos2 = jnp.concatenate([cos, cos], axis=1)
    sin2 = jnp.concatenate([sin, sin], axis=1)
    tail = jnp.zeros((n, HEAD_PAD - MLA_NOPE - MLA_ROPE), F32)
    cosq = jnp.concatenate([jnp.ones((n, MLA_NOPE), F32), cos2, tail], axis=1)
    rot = jnp.concatenate([jnp.zeros((n, MLA_NOPE), F32), sin2, tail], axis=1)
    cosk = jnp.concatenate([jnp.zeros((n, MLA_NOPE), F32), cos2, tail], axis=1)
    return dict(cosqT=cosq.T, sinqT=rot.T, cosk=cosk, sink=rot)


def _route(top_idx, rank, counts, bm):
    T = top_idx.shape[0]
    n_assign = T * TOP_K
    pcounts = (counts + bm - 1) // bm * bm
    pends = jnp.cumsum(pcounts)
    pstarts = pends - pcounts
    onehot = top_idx[:, :, None] == jnp.arange(N_EXPERTS, dtype=jnp.int32)
    dest = jnp.sum(jnp.where(onehot, pstarts, 0), axis=-1) + rank
    n_blocks = -(-n_assign // bm) + N_EXPERTS
    slots = n_blocks * bm
    slot_tok = jnp.zeros((slots,), jnp.int32).at[dest.reshape(-1)].set(
        jnp.arange(n_assign, dtype=jnp.int32) // TOP_K, unique_indices=True, mode="promise_in_bounds")
    block_start = jnp.arange(n_blocks, dtype=jnp.int32) * bm
    block_e = jnp.minimum(jnp.sum(pends[None, :] <= block_start[:, None], axis=1), N_EXPERTS - 1).astype(jnp.int32)
    n_used = (pends[-1:] // bm).astype(jnp.int32)
    return slot_tok, block_e, n_used, dest


def _mixer(x, cnt0, wts, meta):
    B, S, D = x.shape
    tm = min(512, S)
    qkv, lg, g, qT, k, vT = _in_proj(x, wts, _rope_tables(N_META, S), tm)
    o_f, o_b = _gla(qkv, lg, meta["qkv"], meta["lg"], tm)
    mla = _mla(qT, k, vT, meta["k"], meta["vT"])
    h1, h1b, route, gate, cnt = _out_proj(x, o_f, o_b, g, mla, cnt0, wts, tm)
    T = B * S
    return h1.reshape(T, D), h1b.reshape(T, D), route.reshape(T, LANE), gate.reshape(T, LANE), cnt


def _encode(xs_in, wts, meta):
    cnt = jnp.zeros((1, LANE), jnp.int32)
    parts = []
    for x in xs_in:
        h1, h1b, route, gate, cnt = _mixer(x, cnt, wts, meta)
        parts.append((h1, h1b, route, gate))
    h1b = jnp.concatenate([p[1] for p in parts], axis=0)
    route = jnp.concatenate([p[2] for p in parts], axis=0)
    slot_tok, block_e, n_used, dest = _route(
        route[:, :TOP_K], route[:, TOP_K:2 * TOP_K], cnt[0, :N_EXPERTS], MOE_BLOCK)
    xs = h1b.at[slot_tok].get(mode="promise_in_bounds")
    y = _moe_ffn(xs, block_e, n_used, wts)
    ys = [y.at[dest[:, kk]].get(mode="promise_in_bounds") for kk in range(TOP_K)]
    outs, row0 = [], 0
    for x, (h1, _, _, gate) in zip(xs_in, parts):
        tm = min(512, x.shape[1])
        out = _final_ln(h1, gate, ys, row0, wts["ln2_g"], wts["ln2_b"], tm)
        outs.append(out.reshape(x.shape))
        row0 += h1.shape[0]
    return tuple(outs)


def _meta_prep(meta_tokens, wts):
    xm = jnp.zeros((1, LANE, D_MODEL), F32).at[0, :N_META].set(meta_tokens)
    qkv, lg, _, _, k, vT = _in_proj(xm, wts, _rope_tables(0, LANE), LANE)
    front = GLA_CHUNK - N_META
    mq = jnp.zeros((GLA_CHUNK, C_G), F32).at[front:].set(qkv[0, :N_META])
    ml = jnp.zeros((GLA_CHUNK, GK), F32).at[front:].set(lg[0, :N_META, :GK])
    return dict(qkv=mq, lg=ml, k=k[0], vT=vT[0, 0])


def kernel(x_prompt, x_sample, meta_tokens, ln_in_g, ln_in_b, w_in, gla_gate_w_fwd, gla_gate_b_fwd,
           gla_gate_w_bwd, gla_gate_b_bwd, gla_norm_g, mla_q_norm_g, mla_w_uq, mla_kv_norm_g, mla_w_ukv,
           w_out, ln1_g, ln1_b, router_w, router_b, expert_w_gu, expert_b_gu, expert_w_down, expert_b_down,
           ln2_g, ln2_b):
    wts = _prep_weights(dict(
        ln_in_g=ln_in_g, ln_in_b=ln_in_b, w_in=w_in, gla_gate_w_fwd=gla_gate_w_fwd,
        gla_gate_b_fwd=gla_gate_b_fwd, gla_gate_w_bwd=gla_gate_w_bwd, gla_gate_b_bwd=gla_gate_b_bwd,
        gla_norm_g=gla_norm_g, mla_q_norm_g=mla_q_norm_g, mla_w_uq=mla_w_uq, mla_kv_norm_g=mla_kv_norm_g,
        mla_w_ukv=mla_w_ukv, w_out=w_out, ln1_g=ln1_g, ln1_b=ln1_b, router_w=router_w, router_b=router_b,
        expert_w_gu=expert_w_gu, expert_b_gu=expert_b_gu, expert_w_down=expert_w_down,
        expert_b_down=expert_b_down, ln2_g=ln2_g, ln2_b=ln2_b))
    meta = _meta_prep(meta_tokens, wts)
    return _encode((x_prompt, x_sample), wts, meta)
```
